```python
import math
import jax, jax.numpy as jnp
from jax import lax
import numpy as np

D_MODEL = 1024
BATCH = 2
SEQ = 8192
DEPTH = 2

N_A = DEPTH // 2
N_B = DEPTH - N_A

H_A = 4
DK_A = D_MODEL // H_A
DV_A = 2 * DK_A
WIDTH_A = H_A * DV_A
CHUNK = 128
ROPE_BASE_A = 10000.0

H_B = 16
QK_NOPE = 128
QK_ROPE = 64
V_HEAD = 128
Q_LORA = 768
KV_LORA = 512
WIDTH_B = H_B * V_HEAD
Q_BLOCK = 128
ROPE_BASE_B = 10000.0

IN_A = H_A * DK_A * 2 + WIDTH_A + WIDTH_A
IN_B = Q_LORA + WIDTH_B

DEEPNORM_ALPHA = (2.0 * DEPTH) ** 0.25
DEEPNORM_BETA = (8.0 * DEPTH) ** -0.25

kernel_name = "yoco_retention_mla_gated_deepnorm"


def _layer_norm(x, g, b, eps=1e-5):
    xf = x.astype(jnp.float32)
    mu = jnp.mean(xf, axis=-1, keepdims=True)
    var = jnp.mean(jnp.square(xf - mu), axis=-1, keepdims=True)
    return ((xf - mu) * lax.rsqrt(var + eps)).astype(x.dtype) * g + b


def _rms_norm(x, g, eps=1e-6):
    xf = x.astype(jnp.float32)
    return (xf * lax.rsqrt(jnp.mean(jnp.square(xf), axis=-1, keepdims=True) + eps)).astype(x.dtype) * g


def _rope(x, base):
    s, d = x.shape[1], x.shape[-1]
    half = d // 2
    inv = base ** (-jnp.arange(half, dtype=jnp.float32) / half)
    ang = jnp.arange(s, dtype=jnp.float32)[:, None] * inv[None, :]
    shp = (s,) + (1,) * (x.ndim - 3) + (half,)
    cos = jnp.cos(ang).reshape(shp).astype(x.dtype)
    sin = jnp.sin(ang).reshape(shp).astype(x.dtype)
    x1, x2 = x[..., :half], x[..., half:]
    return jnp.concatenate([x1 * cos - x2 * sin, x2 * cos + x1 * sin], axis=-1)


def _chunkwise_retention(q, k, v):
    b, s, h, dk = q.shape
    dv = v.shape[-1]
    n = s // CHUNK

    def to_chunks(t):
        return t.astype(jnp.float32).reshape(b, n, CHUNK, h, t.shape[-1]).transpose(1, 0, 3, 2, 4)

    qc, kc, vc = to_chunks(q), to_chunks(k), to_chunks(v)
    lg = jnp.log1p(-jnp.exp2(-5.0 - jnp.arange(h, dtype=jnp.float32)))
    idx = jnp.arange(CHUNK, dtype=jnp.float32)
    diff = idx[:, None] - idx[None, :]
    causal = diff >= 0
    dmat = jnp.where(causal, jnp.exp(jnp.where(causal, diff, 0.0)[None] * lg[:, None, None]), 0.0)
    qdec = jnp.exp((idx + 1.0)[None, :] * lg[:, None])
    kdec = jnp.exp((CHUNK - 1.0 - idx)[None, :] * lg[:, None])
    cdec = jnp.exp(CHUNK * lg)

    def step(state, inp):
        qi, ki, vi = inp
        scores = jnp.einsum('bhcd,bhed->bhce', qi, ki) * dmat
        inner = jnp.einsum('bhce,bhev->bhcv', scores, vi)
        cross = jnp.einsum('bhcd,bhdv->bhcv', qi * qdec[None, :, :, None], state)
        state = state * cdec[None, :, None, None] + jnp.einsum(
            'bhcd,bhcv->bhdv', ki * kdec[None, :, :, None], vi)
        return state, inner + cross

    init = jnp.zeros((b, h, dk, dv), jnp.float32)
    _, out = lax.scan(step, init, (qc, kc, vc))
    return out.transpose(1, 0, 3, 2, 4).reshape(b, s, h, dv)


def _retention_layer(x, w_in, w_out):
    b, s, _ = x.shape
    hproj = x @ w_in
    qk = H_A * DK_A
    q = hproj[..., :qk].reshape(b, s, H_A, DK_A)
    k = hproj[..., qk:2 * qk].reshape(b, s, H_A, DK_A)
    v = hproj[..., 2 * qk:2 * qk + WIDTH_A].reshape(b, s, H_A, DV_A)
    gate = hproj[..., 2 * qk + WIDTH_A:]
    q = _rope(q, ROPE_BASE_A)
    k = _rope(k, ROPE_BASE_A) * (DK_A ** -0.5)
    o = _chunkwise_retention(q, k, v)
    mu = jnp.mean(o, axis=-1, keepdims=True)
    var = jnp.mean(jnp.square(o - mu), axis=-1, keepdims=True)
    o = ((o - mu) * lax.rsqrt(var + 1e-5)).astype(x.dtype).reshape(b, s, WIDTH_A)
    return (o * jax.nn.silu(gate)) @ w_out


def _shared_latent_kv(x, w_down, kv_norm, w_up):
    b, s, _ = x.shape
    c = x @ w_down
    lat = _rms_norm(c[..., :KV_LORA], kv_norm)
    k_rope = _rope(c[..., KV_LORA:], ROPE_BASE_B)
    kv = (lat @ w_up).reshape(b, s, H_B, QK_NOPE + V_HEAD)
    return kv[..., :QK_NOPE], k_rope, kv[..., QK_NOPE:]


def _causal_block_attention(q_nope, q_rope, k_nope, k_rope, v):
    b, s, h, _ = q_nope.shape
    nb = s // Q_BLOCK
    scale = (QK_NOPE + QK_ROPE) ** -0.5
    qn_b = q_nope.reshape(b, nb, Q_BLOCK, h, QK_NOPE).transpose(1, 0, 2, 3, 4)
    qr_b = q_rope.reshape(b, nb, Q_BLOCK, h, QK_ROPE).transpose(1, 0, 2, 3, 4)
    kpos = jnp.arange(s)

    def block(args):
        qn_i, qr_i, i = args
        sc = (jnp.einsum('bqhd,bkhd->bhqk', qn_i, k_nope)
              + jnp.einsum('bqhr,bkr->bhqk', qr_i, k_rope)).astype(jnp.float32) * scale
        qpos = i * Q_BLOCK + jnp.arange(Q_BLOCK)
        sc = jnp.where(kpos[None, :] <= qpos[:, None], sc, -jnp.inf)
        p = jax.nn.softmax(sc, axis=-1).astype(v.dtype)
        return jnp.einsum('bhqk,bkhd->bqhd', p, v)

    out = lax.map(block, (qn_b, qr_b, jnp.arange(nb)))
    return out.transpose(1, 0, 2, 3, 4).reshape(b, s, h, V_HEAD)


def _mla_layer(x, w_in, q_norm, w_uq, w_out, k_nope, k_rope, v):
    b, s, _ = x.shape
    hproj = x @ w_in
    q_lat, gate = hproj[..., :Q_LORA], hproj[..., Q_LORA:]
    q = (_rms_norm(q_lat, q_norm) @ w_uq).reshape(b, s, H_B, QK_NOPE + QK_ROPE)
    q_nope = q[..., :QK_NOPE]
    q_rope = _rope(q[..., QK_NOPE:], ROPE_BASE_B)
    o = _causal_block_attention(q_nope, q_rope, k_nope, k_rope, v).reshape(b, s, WIDTH_B)
    return (o * jax.nn.silu(gate)) @ w_out


def setup_inputs(seed: int = 0) -> dict:
    key = jax.random.key(seed)
    ks = jax.random.split(key, 16)
    nrm = jax.random.normal
    f32 = jnp.float32
    return {
        "x": nrm(ks[0], (BATCH, SEQ, D_MODEL), f32),
        "a_w_in": nrm(ks[1], (N_A, D_MODEL, IN_A), f32) * D_MODEL ** -0.5,
        "a_w_out": nrm(ks[2], (N_A, WIDTH_A, D_MODEL), f32) * (WIDTH_A ** -0.5 * DEEPNORM_BETA),
        "b_w_in": nrm(ks[3], (N_B, D_MODEL, IN_B), f32) * D_MODEL ** -0.5,
        "b_q_norm": 1.0 + 0.02 * nrm(ks[4], (N_B, Q_LORA), f32),
        "b_w_uq": nrm(ks[5], (N_B, Q_LORA, H_B * (QK_NOPE + QK_ROPE)), f32) * Q_LORA ** -0.5,
        "b_w_out": nrm(ks[6], (N_B, WIDTH_B, D_MODEL), f32) * (WIDTH_B ** -0.5 * DEEPNORM_BETA),
        "kv_w_down": nrm(ks[7], (D_MODEL, KV_LORA + QK_ROPE), f32) * D_MODEL ** -0.5,
        "kv_norm": 1.0 + 0.02 * nrm(ks[8], (KV_LORA,), f32),
        "kv_w_up": nrm(ks[9], (KV_LORA, H_B * (QK_NOPE + V_HEAD)), f32) * KV_LORA ** -0.5,
        "ln_g": 1.0 + 0.02 * nrm(ks[10], (DEPTH, D_MODEL), f32),
        "ln_b": 0.02 * nrm(ks[11], (DEPTH, D_MODEL), f32),
    }


def reference(x, a_w_in, a_w_out, b_w_in, b_q_norm, b_w_uq, b_w_out,
              kv_w_down, kv_norm, kv_w_up, ln_g, ln_b):
    shared = None
    for layer in range(DEPTH):
        if layer < N_A:
            y = _retention_layer(x, a_w_in[layer], a_w_out[layer])
        else:
            j = layer - N_A
            if j == 0:
                shared = _shared_latent_kv(x, kv_w_down, kv_norm, kv_w_up)
            y = _mla_layer(x, b_w_in[j], b_q_norm[j], b_w_uq[j], b_w_out[j], *shared)
        x = _layer_norm(DEEPNORM_ALPHA * x + y, ln_g[layer], ln_b[layer])
    return x
```

```python
import functools
import math

import jax
import jax.numpy as jnp
from jax import lax
from jax.experimental import pallas as pl
from jax.experimental.pallas import tpu as pltpu

D_MODEL = 1024
DEPTH = 2

H_A = 4
DK_A = D_MODEL // H_A
DV_A = 2 * DK_A
WIDTH_A = H_A * DV_A
ROPE_BASE_A = 10000.0

H_B = 16
QK_NOPE = 128
QK_ROPE = 64
V_HEAD = 128
Q_LORA = 768
KV_LORA = 512
WIDTH_B = H_B * V_HEAD
ROPE_BASE_B = 10000.0
QK_PAD = 256

DEEPNORM_ALPHA = (2.0 * DEPTH) ** 0.25

RET_CHUNK = 256
ROW_TILE = 512
ATTN_TQ = 512
ATTN_TK = 512
MASK_VALUE = -1e30

VMEM_LIMIT_BYTES = 56 * 1024 * 1024

BF16 = jnp.bfloat16
F32 = jnp.float32


def _params(*semantics):
    return pltpu.CompilerParams(dimension_semantics=semantics,
                                vmem_limit_bytes=VMEM_LIMIT_BYTES)


def _resident(shape):
    zeros = (0,) * len(shape)
    return pl.BlockSpec(shape, lambda *_: zeros, pipeline_mode=pl.Buffered(1))


def _dot(a, b):
    return jnp.dot(a, b, preferred_element_type=F32)


def _dot_nt(a, b):
    return lax.dot_general(a, b, (((1,), (1,)), ((), ())), preferred_element_type=F32)


def _dot_tn(a, b):
    return lax.dot_general(a, b, (((0,), (0,)), ((), ())), preferred_element_type=F32)


def _silu(g):
    return g / (1.0 + jnp.exp(-g))


def _ret_inproj_kernel(x_ref, w_ref, cos_ref, sin_ref, q_ref, k_ref, v_ref, sg_ref):
    xb = x_ref[...].astype(BF16)
    cos = cos_ref[...]
    sin = sin_ref[...]
    qk = H_A * DK_A
    half = DK_A // 2

    def rope_store(h, out_ref, mult):
        for i in range(H_A):
            x1 = h[:, i * DK_A:i * DK_A + half]
            x2 = h[:, i * DK_A + half:(i + 1) * DK_A]
            out_ref[:, i * DK_A:i * DK_A + half] = ((x1 * cos - x2 * sin) * mult).astype(BF16)
            out_ref[:, i * DK_A + half:(i + 1) * DK_A] = ((x2 * cos + x1 * sin) * mult).astype(BF16)

    rope_store(_dot(xb, w_ref[:, 0:qk]), q_ref, 1.0)
    rope_store(_dot(xb, w_ref[:, qk:2 * qk]), k_ref, DK_A ** -0.5)
    v_ref[...] = _dot(xb, w_ref[:, 2 * qk:2 * qk + WIDTH_A]).astype(BF16)
    sg_ref[...] = _silu(_dot(xb, w_ref[:, 2 * qk + WIDTH_A:])).astype(BF16)


def _ret_inproj(x2, w, cos, sin, seq):
    t = x2.shape[0]
    tm = ROW_TILE
    nseq = seq // tm
    qk = H_A * DK_A
    row = lambda i: (i, 0)
    pos = lambda i: (i % nseq, 0)
    return pl.pallas_call(
        _ret_inproj_kernel,
        grid=(t // tm,),
        in_specs=[pl.BlockSpec((tm, D_MODEL), row),
                  _resident(w.shape),
                  pl.BlockSpec((tm, DK_A // 2), pos),
                  pl.BlockSpec((tm, DK_A // 2), pos)],
        out_specs=[pl.BlockSpec((tm, qk), row), pl.BlockSpec((tm, qk), row),
                   pl.BlockSpec((tm, WIDTH_A), row), pl.BlockSpec((tm, WIDTH_A), row)],
        out_shape=[jax.ShapeDtypeStruct((t, qk), BF16), jax.ShapeDtypeStruct((t, qk), BF16),
                   jax.ShapeDtypeStruct((t, WIDTH_A), BF16), jax.ShapeDtypeStruct((t, WIDTH_A), BF16)],
        compiler_params=_params("arbitrary"),
        name="ret_inproj",
    )(x2, w, cos, sin)


def _retention_kernel(q_ref, k_ref, v_ref, sg_ref, dmat_ref, qdec_ref, kdec_ref, cdec_ref,
                      o_ref, state_ref):
    @pl.when(pl.program_id(2) == 0)
    def _():
        state_ref[...] = jnp.zeros_like(state_ref)

    q = q_ref[0]
    k = k_ref[0]
    v = v_ref[0]
    state = state_ref[...]
    scores = (_dot_nt(q, k) * dmat_ref[0]).astype(BF16)
    qd = (q.astype(F32) * qdec_ref[0]).astype(BF16)
    o = _dot(scores, v) + _dot(qd, state.astype(BF16))
    kd = (k.astype(F32) * kdec_ref[0]).astype(BF16)
    state_ref[...] = state * cdec_ref[0] + _dot_tn(kd, v)

    mu = jnp.mean(o, axis=-1, keepdims=True)
    d = o - mu
    var = jnp.mean(d * d, axis=-1, keepdims=True)
    o_ref[0] = (d * lax.rsqrt(var + 1e-5) * sg_ref[0].astype(F32)).astype(BF16)


def _retention(q, k, v, sg, dmat, qdec, kdec, cdec):
    b, s, _ = q.shape
    c = RET_CHUNK
    tok = lambda bi, h, ci: (bi, ci, h)
    head = lambda bi, h, ci: (h, 0, 0)
    return pl.pallas_call(
        _retention_kernel,
        grid=(b, H_A, s // c),
        in_specs=[pl.BlockSpec((1, c, DK_A), tok), pl.BlockSpec((1, c, DK_A), tok),
                  pl.BlockSpec((1, c, DV_A), tok), pl.BlockSpec((1, c, DV_A), tok),
                  pl.BlockSpec((1, c, c), head), pl.BlockSpec((1, c, 1), head),
                  pl.BlockSpec((1, c, 1), head), pl.BlockSpec((1, 1, DV_A), head)],
        out_specs=pl.BlockSpec((1, c, DV_A), tok),
        out_shape=jax.ShapeDtypeStruct((b, s, WIDTH_A), BF16),
        scratch_shapes=[pltpu.VMEM((DK_A, DV_A), F32)],
        compiler_params=_params("arbitrary", "arbitrary", "arbitrary"),
        name="retention",
    )(q, k, v, sg, dmat, qdec, kdec, cdec)


def _outproj_ln_kernel(a_ref, w_ref, x_ref, g_ref, b_ref, o_ref):
    z = DEEPNORM_ALPHA * x_ref[...] + _dot(a_ref[...], w_ref[...])
    mu = jnp.mean(z, axis=-1, keepdims=True)
    d = z - mu
    var = jnp.mean(d * d, axis=-1, keepdims=True)
    o_ref[...] = d * lax.rsqrt(var + 1e-5) * g_ref[...] + b_ref[...]


def _outproj_ln(a, w, x2, g, bias):
    t, width = a.shape
    tm = ROW_TILE
    row = lambda i: (i, 0)
    return pl.pallas_call(
        _outproj_ln_kernel,
        grid=(t // tm,),
        in_specs=[pl.BlockSpec((tm, width), row), _resident(w.shape),
                  pl.BlockSpec((tm, D_MODEL), row), _resident(g.shape), _resident(bias.shape)],
        out_specs=pl.BlockSpec((tm, D_MODEL), row),
        out_shape=jax.ShapeDtypeStruct((t, D_MODEL), F32),
        compiler_params=_params("arbitrary"),
        name="outproj_ln",
    )(a, w, x2, g, bias)


def _rope_pair(u, tab):
    t = u * tab
    lane = lax.broadcasted_iota(jnp.int32, t.shape, 1)
    return jnp.where(lane < QK_ROPE, t + pltpu.roll(t, QK_ROPE, 1), 0.0)


def _rms(x, g, eps=1e-6):
    return x * lax.rsqrt(jnp.mean(x * x, axis=-1, keepdims=True) + eps) * g


def _mla_inproj_kernel(x_ref, win_ref, wdn_ref, qg_ref, kvg_ref, tab_ref,
                       qn_ref, sg_ref, lat_ref, kr_ref):
    xb = x_ref[...].astype(BF16)
    qn_ref[...] = _rms(_dot(xb, win_ref[:, 0:Q_LORA]), qg_ref[...]).astype(BF16)
    sg_ref[...] = _silu(_dot(xb, win_ref[:, Q_LORA:])).astype(BF16)
    c = _dot(xb, wdn_ref[...])
    lat_ref[...] = _rms(c[:, 0:KV_LORA], kvg_ref[...]).astype(BF16)
    kr_ref[...] = _rope_pair(c[:, KV_LORA:], tab_ref[...]).astype(BF16)


def _mla_inproj(x2, win, wdn, qg, kvg, tab, seq):
    t = x2.shape[0]
    tm = ROW_TILE
    nseq = seq // tm
    row = lambda i: (i, 0)
    pos = lambda i: (i % nseq, 0)
    widths = (Q_LORA, WIDTH_B, KV_LORA, 2 * QK_ROPE)
    return pl.pallas_call(
        _mla_inproj_kernel,
        grid=(t // tm,),
        in_specs=[pl.BlockSpec((tm, D_MODEL), row), _resident(win.shape), _resident(wdn.shape),
                  _resident(qg.shape), _resident(kvg.shape),
                  pl.BlockSpec((tm, 2 * QK_ROPE), pos)],
        out_specs=[pl.BlockSpec((tm, w), row) for w in widths],
        out_shape=[jax.ShapeDtypeStruct((t, w), BF16) for w in widths],
        compiler_params=_params("arbitrary"),
        name="mla_inproj",
    )(x2, win, wdn, qg, kvg, tab)


def _q_up_kernel(qn_ref, w_ref, tab_ref, q_ref):
    qn = qn_ref[...]
    tab = tab_ref[...]
    scale = (QK_NOPE + QK_ROPE) ** -0.5
    for h in range(H_B):
        y = _dot(qn, w_ref[:, h * QK_PAD:(h + 1) * QK_PAD])
        q_ref[0, h, :, 0:QK_NOPE] = (y[:, 0:QK_NOPE] * scale).astype(BF16)
        q_ref[0, h, :, QK_NOPE:] = (_rope_pair(y[:, QK_NOPE:], tab) * scale).astype(BF16)


def _q_up(qn, w, tab, batch, seq):
    tm = ROW_TILE
    nseq = seq // tm
    return pl.pallas_call(
        _q_up_kernel,
        grid=(batch, nseq),
        in_specs=[pl.BlockSpec((tm, Q_LORA), lambda b, i: (b * nseq + i, 0)),
                  _resident(w.shape),
                  pl.BlockSpec((tm, 2 * QK_ROPE), lambda b, i: (i, 0))],
        out_specs=pl.BlockSpec((1, H_B, tm, QK_PAD), lambda b, i: (b, 0, i, 0)),
        out_shape=jax.ShapeDtypeStruct((batch, H_B, seq, QK_PAD), BF16),
        compiler_params=_params("arbitrary", "arbitrary"),
        name="q_up",
    )(qn, w, tab)


def _kv_up_kernel(lat_ref, kr_ref, wk_ref, wvt_ref, k_ref, vt_ref):
    lat = lat_ref[...]
    kn = _dot(lat, wk_ref[...]).astype(BF16)
    kr = kr_ref[...]
    for h in range(H_B):
        k_ref[0, h, :, 0:QK_NOPE] = kn[:, h * QK_NOPE:(h + 1) * QK_NOPE]
        k_ref[0, h, :, QK_NOPE:] = kr
    vt = _dot_nt(wvt_ref[...], lat).astype(BF16)
    vt_ref[0, :, 0] = vt.reshape(H_B, V_HEAD, vt.shape[-1])


def _kv_up(lat, kr, wk, wvt, batch, seq):
    tk = ATTN_TK
    nk = seq // tk
    row = lambda b, i: (b * nk + i, 0)
    return pl.pallas_call(
        _kv_up_kernel,
        grid=(batch, nk),
        in_specs=[pl.BlockSpec((tk, KV_LORA), row), pl.BlockSpec((tk, 2 * QK_ROPE), row),
                  _resident(wk.shape), _resident(wvt.shape)],
        out_specs=[pl.BlockSpec((1, H_B, tk, QK_PAD), lambda b, i: (b, 0, i, 0)),
                   pl.BlockSpec((1, H_B, 1, V_HEAD, tk), lambda b, i: (b, 0, i, 0, 0))],
        out_shape=[jax.ShapeDtypeStruct((batch, H_B, seq, QK_PAD), BF16),
                   jax.ShapeDtypeStruct((batch, H_B, nk, V_HEAD, tk), BF16)],
        compiler_params=_params("arbitrary", "arbitrary"),
        name="kv_up",
    )(lat, kr, wk, wvt)


def _attn_kernel(q_ref, k_ref, vt_ref, sg_ref, o_ref):
    qi = pl.program_id(2)
    q = q_ref[0, 0]
    tq = q.shape[0]
    tk = ATTN_TK

    def step(ki, carry, masked):
        m, l, acc = carry
        kb = k_ref[0, 0, pl.ds(pl.multiple_of(ki * tk, tk), tk), :]
        s = _dot_nt(kb, q)
        if masked:
            kpos = lax.broadcasted_iota(jnp.int32, s.shape, 0)
            qpos = lax.broadcasted_iota(jnp.int32, s.shape, 1)
            s = jnp.where(kpos <= qpos, s, MASK_VALUE)
        m_new = jnp.maximum(m, jnp.max(s, axis=0, keepdims=True))
        alpha = jnp.exp(m - m_new)
        p = jnp.exp(s - m_new)
        l = alpha * l + jnp.sum(p, axis=0, keepdims=True)
        acc = alpha * acc + _dot(vt_ref[0, 0, ki], p.astype(BF16))
        return m_new, l, acc

    init = (jnp.full((1, tq), MASK_VALUE, F32), jnp.zeros((1, tq), F32),
            jnp.zeros((V_HEAD, tq), F32))
    carry = lax.fori_loop(0, qi, lambda ki, c: step(ki, c, False), init)
    _, l, acc = step(qi, carry, True)
    o = (acc / l).T
    o_ref[0] = (o * sg_ref[0].astype(F32)).astype(BF16)


def _attention(q, k, vt, sg3):
    b, h, s, _ = q.shape
    tq = ATTN_TQ
    assert tq == ATTN_TK
    nk = s // ATTN_TK
    return pl.pallas_call(
        _attn_kernel,
        grid=(b, h, s // tq),
        in_specs=[pl.BlockSpec((1, 1, tq, QK_PAD), lambda bi, hi, qi: (bi, hi, qi, 0)),
                  pl.BlockSpec((1, 1, s, QK_PAD), lambda bi, hi, qi: (bi, hi, 0, 0)),
                  pl.BlockSpec((1, 1, nk, V_HEAD, ATTN_TK), lambda bi, hi, qi: (bi, hi, 0, 0, 0)),
                  pl.BlockSpec((1, tq, V_HEAD), lambda bi, hi, qi: (bi, qi, hi))],
        out_specs=pl.BlockSpec((1, tq, V_HEAD), lambda bi, hi, qi: (bi, qi, hi)),
        out_shape=jax.ShapeDtypeStruct((b, s, WIDTH_B), BF16),
        compiler_params=_params("arbitrary", "arbitrary", "arbitrary"),
        name="mla_attention",
    )(q, k, vt, sg3)


def _rope_angles(seq, half, base):
    inv = base ** (-jnp.arange(half, dtype=F32) / half)
    return jnp.arange(seq, dtype=F32)[:, None] * inv[None, :]


def _retention_decays(c):
    lg = jnp.log1p(-jnp.exp2(-5.0 - jnp.arange(H_A, dtype=F32)))
    idx = jnp.arange(c, dtype=F32)
    diff = idx[:, None] - idx[None, :]
    causal = diff >= 0
    dmat = jnp.where(causal, jnp.exp(jnp.where(causal, diff, 0.0)[None] * lg[:, None, None]), 0.0)
    qdec = jnp.exp((idx + 1.0)[None, :] * lg[:, None])[:, :, None]
    kdec = jnp.exp((c - 1.0 - idx)[None, :] * lg[:, None])[:, :, None]
    cdec = jnp.broadcast_to(jnp.exp(c * lg)[:, None, None], (H_A, 1, DV_A))
    return dmat, qdec, kdec, cdec


def _swap_halves(w):
    half = w.shape[-1] // 2
    return jnp.concatenate([w[..., half:], w[..., :half]], axis=-1)


def kernel(x, a_w_in, a_w_out, b_w_in, b_q_norm, b_w_uq, b_w_out, kv_w_down, kv_norm, kv_w_up,
           ln_g, ln_b):
    batch, seq, _ = x.shape
    t = batch * seq
    x2 = x.reshape(t, D_MODEL)

    ang_a = _rope_angles(seq, DK_A // 2, ROPE_BASE_A)
    cos_a, sin_a = jnp.cos(ang_a), jnp.sin(ang_a)
    ang_b = _rope_angles(seq, QK_ROPE // 2, ROPE_BASE_B)
    cos_b, sin_b = jnp.cos(ang_b), jnp.sin(ang_b)
    tab_b = jnp.concatenate([cos_b, cos_b, -sin_b, sin_b], axis=-1)

    q, k, v, sg = _ret_inproj(x2, a_w_in[0].astype(BF16), cos_a, sin_a, seq)
    shape3 = lambda a: a.reshape(batch, seq, a.shape[-1])
    o = _retention(shape3(q), shape3(k), shape3(v), shape3(sg), *_retention_decays(RET_CHUNK))
    x1 = _outproj_ln(o.reshape(t, WIDTH_A), a_w_out[0].astype(BF16), x2,
                     ln_g[0][None, :], ln_b[0][None, :])

    w_rope = kv_w_down[:, KV_LORA:]
    wdn = jnp.concatenate([kv_w_down[:, :KV_LORA], w_rope, _swap_halves(w_rope)], axis=-1)
    qn, sgb, lat, kr = _mla_inproj(x1, b_w_in[0].astype(BF16), wdn.astype(BF16),
                                   b_q_norm[0][None, :], kv_norm[None, :], tab_b, seq)

    wuq = b_w_uq[0].reshape(Q_LORA, H_B, QK_NOPE + QK_ROPE)
    wuq = jnp.concatenate([wuq, _swap_halves(wuq[..., QK_NOPE:])], axis=-1)
    qh = _q_up(qn, wuq.reshape(Q_LORA, H_B * QK_PAD).astype(BF16), tab_b, batch, seq)

    wup = kv_w_up.reshape(KV_LORA, H_B, QK_NOPE + V_HEAD)
    wk = wup[..., :QK_NOPE].reshape(KV_LORA, H_B * QK_NOPE).astype(BF16)
    wvt = wup[..., QK_NOPE:].reshape(KV_LORA, H_B * V_HEAD).T.astype(BF16)
    kh, vt = _kv_up(lat, kr, wk, wvt, batch, seq)

    ob = _attention(qh, kh, vt, sgb.reshape(batch, seq, WIDTH_B))
    out = _outproj_ln(ob.reshape(t, WIDTH_B), b_w_out[0].astype(BF16), x1,
                      ln_g[1][None, :], ln_b[1][None, :])
    return out.reshape(batch, seq, D_MODEL)
```

```python
import functools
import math

import jax
import jax.numpy as jnp
from jax import lax
from jax.experimental import pallas as pl
from jax.experimental.pallas import tpu as pltpu

D_MODEL = 1024
DEPTH = 2

H_A = 4
DK_A = D_MODEL // H_A
DV_A = 2 * DK_A
WIDTH_A = H_A * DV_A
ROPE_BASE_A = 10000.0

H_B = 16
QK_NOPE = 128
QK_ROPE = 64
V_HEAD = 128
Q_LORA = 768
KV_LORA = 512
WIDTH_B = H_B * V_HEAD
ROPE_BASE_B = 10000.0
QK_PAD = 256

DEEPNORM_ALPHA = (2.0 * DEPTH) ** 0.25

RET_CHUNK = 256
ROW_TILE = 512
ATTN_TQ = 512
ATTN_TK = 512
MASK_VALUE = -1e30

VMEM_LIMIT_BYTES = 56 * 1024 * 1024

BF16 = jnp.bfloat16
F32 = jnp.float32


def _params(*semantics):
    return pltpu.CompilerParams(dimension_semantics=semantics,
                                vmem_limit_bytes=VMEM_LIMIT_BYTES)


def _resident(shape):
    zeros = (0,) * len(shape)
    return pl.BlockSpec(shape, lambda *_: zeros, pipeline_mode=pl.Buffered(1))


def _dot(a, b):
    return jnp.dot(a, b, preferred_element_type=F32)


def _dot_nt(a, b):
    return lax.dot_general(a, b, (((1,), (1,)), ((), ())), preferred_element_type=F32)


def _dot_tn(a, b):
    return lax.dot_general(a, b, (((0,), (0,)), ((), ())), preferred_element_type=F32)


def _silu(g):
    return g / (1.0 + jnp.exp(-g))


def _ret_inproj_kernel(x_ref, w_ref, cos_ref, sin_ref, q_ref, k_ref, v_ref, sg_ref):
    xb = x_ref[...].astype(BF16)
    cos = cos_ref[...]
    sin = sin_ref[...]
    qk = H_A * DK_A
    half = DK_A // 2

    def rope_store(h, out_ref, mult):
        for i in range(H_A):
            x1 = h[:, i * DK_A:i * DK_A + half]
            x2 = h[:, i * DK_A + half:(i + 1) * DK_A]
            out_ref[:, i * DK_A:i * DK_A + half] = ((x1 * cos - x2 * sin) * mult).astype(BF16)
            out_ref[:, i * DK_A + half:(i + 1) * DK_A] = ((x2 * cos + x1 * sin) * mult).astype(BF16)

    rope_store(_dot(xb, w_ref[:, 0:qk]), q_ref, 1.0)
    rope_store(_dot(xb, w_ref[:, qk:2 * qk]), k_ref, DK_A ** -0.5)
    v_ref[...] = _dot(xb, w_ref[:, 2 * qk:2 * qk + WIDTH_A]).astype(BF16)
    sg_ref[...] = _silu(_dot(xb, w_ref[:, 2 * qk + WIDTH_A:])).astype(BF16)


def _ret_inproj(x2, w, cos, sin, seq):
    t = x2.shape[0]
    tm = ROW_TILE
    nseq = seq // tm
    qk = H_A * DK_A
    row = lambda i: (i, 0)
    pos = lambda i: (i % nseq, 0)
    return pl.pallas_call(
        _ret_inproj_kernel,
        grid=(t // tm,),
        in_specs=[pl.BlockSpec((tm, D_MODEL), row),
                  _resident(w.shape),
                  pl.BlockSpec((tm, DK_A // 2), pos),
                  pl.BlockSpec((tm, DK_A // 2), pos)],
        out_specs=[pl.BlockSpec((tm, qk), row), pl.BlockSpec((tm, qk), row),
                   pl.BlockSpec((tm, WIDTH_A), row), pl.BlockSpec((tm, WIDTH_A), row)],
        out_shape=[jax.ShapeDtypeStruct((t, qk), BF16), jax.ShapeDtypeStruct((t, qk), BF16),
                   jax.ShapeDtypeStruct((t, WIDTH_A), BF16), jax.ShapeDtypeStruct((t, WIDTH_A), BF16)],
        compiler_params=_params("arbitrary"),
        name="ret_inproj",
    )(x2, w, cos, sin)


def _retention_kernel(q_ref, k_ref, v_ref, sg_ref, dmat_ref, qdec_ref, kdec_ref, cdec_ref,
                      o_ref, state_ref):
    @pl.when(pl.program_id(2) == 0)
    def _():
        state_ref[...] = jnp.zeros_like(state_ref)

    q = q_ref[0]
    k = k_ref[0]
    v = v_ref[0]
    state = state_ref[...]
    scores = (_dot_nt(q, k) * dmat_ref[0]).astype(BF16)
    qd = (q.astype(F32) * qdec_ref[0]).astype(BF16)
    o = _dot(scores, v) + _dot(qd, state.astype(BF16))
    kd = (k.astype(F32) * kdec_ref[0]).astype(BF16)
    state_ref[...] = state * cdec_ref[0] + _dot_tn(kd, v)

    mu = jnp.mean(o, axis=-1, keepdims=True)
    d = o - mu
    var = jnp.mean(d * d, axis=-1, keepdims=True)
    o_ref[0] = (d * lax.rsqrt(var + 1e-5) * sg_ref[0].astype(F32)).astype(BF16)


def _retention(q, k, v, sg, dmat, qdec, kdec, cdec):
    b, s, _ = q.shape
    c = RET_CHUNK
    tok = lambda bi, h, ci: (bi, ci, h)
    head = lambda bi, h, ci: (h, 0, 0)
    return pl.pallas_call(
        _retention_kernel,
        grid=(b, H_A, s // c),
        in_specs=[pl.BlockSpec((1, c, DK_A), tok), pl.BlockSpec((1, c, DK_A), tok),
                  pl.BlockSpec((1, c, DV_A), tok), pl.BlockSpec((1, c, DV_A), tok),
                  pl.BlockSpec((1, c, c), head), pl.BlockSpec((1, c, 1), head),
                  pl.BlockSpec((1, c, 1), head), pl.BlockSpec((1, 1, DV_A), head)],
        out_specs=pl.BlockSpec((1, c, DV_A), tok),
        out_shape=jax.ShapeDtypeStruct((b, s, WIDTH_A), BF16),
        scratch_shapes=[pltpu.VMEM((DK_A, DV_A), F32)],
        compiler_params=_params("arbitrary", "arbitrary", "arbitrary"),
        name="retention",
    )(q, k, v, sg, dmat, qdec, kdec, cdec)


def _outproj_ln_kernel(a_ref, w_ref, x_ref, g_ref, b_ref, o_ref):
    z = DEEPNORM_ALPHA * x_ref[...] + _dot(a_ref[...], w_ref[...])
    mu = jnp.mean(z, axis=-1, keepdims=True)
    d = z - mu
    var = jnp.mean(d * d, axis=-1, keepdims=True)
    o_ref[...] = d * lax.rsqrt(var + 1e-5) * g_ref[...] + b_ref[...]


def _outproj_ln(a, w, x2, g, bias):
    t, width = a.shape
    tm = ROW_TILE
    row = lambda i: (i, 0)
    return pl.pallas_call(
        _outproj_ln_kernel,
        grid=(t // tm,),
        in_specs=[pl.BlockSpec((tm, width), row), _resident(w.shape),
                  pl.BlockSpec((tm, D_MODEL), row), _resident(g.shape), _resident(bias.shape)],
        out_specs=pl.BlockSpec((tm, D_MODEL), row),
        out_shape=jax.ShapeDtypeStruct((t, D_MODEL), F32),
        compiler_params=_params("arbitrary"),
        name="outproj_ln",
    )(a, w, x2, g, bias)


def _rope_pair(u, tab):
    t = u * tab
    lane = lax.broadcasted_iota(jnp.int32, t.shape, 1)
    return jnp.where(lane < QK_ROPE, t + pltpu.roll(t, QK_ROPE, 1), 0.0)


def _rms(x, g, eps=1e-6):
    return x * lax.rsqrt(jnp.mean(x * x, axis=-1, keepdims=True) + eps) * g


def _mla_inproj_kernel(x_ref, win_ref, wdn_ref, qg_ref, kvg_ref, tab_ref,
                       qn_ref, sg_ref, lat_ref, kr_ref):
    xb = x_ref[...].astype(BF16)
    qn_ref[...] = _rms(_dot(xb, win_ref[:, 0:Q_LORA]), qg_ref[...]).astype(BF16)
    sg_ref[...] = _silu(_dot(xb, win_ref[:, Q_LORA:])).astype(BF16)
    c = _dot(xb, wdn_ref[...])
    lat_ref[...] = _rms(c[:, 0:KV_LORA], kvg_ref[...]).astype(BF16)
    kr_ref[...] = _rope_pair(c[:, KV_LORA:], tab_ref[...]).astype(BF16)


def _mla_inproj(x2, win, wdn, qg, kvg, tab, seq):
    t = x2.shape[0]
    tm = ROW_TILE
    nseq = seq // tm
    row = lambda i: (i, 0)
    pos = lambda i: (i % nseq, 0)
    widths = (Q_LORA, WIDTH_B, KV_LORA, 2 * QK_ROPE)
    return pl.pallas_call(
        _mla_inproj_kernel,
        grid=(t // tm,),
        in_specs=[pl.BlockSpec((tm, D_MODEL), row), _resident(win.shape), _resident(wdn.shape),
                  _resident(qg.shape), _resident(kvg.shape),
                  pl.BlockSpec((tm, 2 * QK_ROPE), pos)],
        out_specs=[pl.BlockSpec((tm, w), row) for w in widths],
        out_shape=[jax.ShapeDtypeStruct((t, w), BF16) for w in widths],
        compiler_params=_params("arbitrary"),
        name="mla_inproj",
    )(x2, win, wdn, qg, kvg, tab)


def _q_up_kernel(qn_ref, w_ref, tab_ref, q_ref):
    qn = qn_ref[...]
    tab = tab_ref[...]
    scale = (QK_NOPE + QK_ROPE) ** -0.5 * math.log2(math.e)
    for h in range(H_B):
        y = _dot(qn, w_ref[:, h * QK_PAD:(h + 1) * QK_PAD])
        q_ref[0, h, :, 0:QK_NOPE] = (y[:, 0:QK_NOPE] * scale).astype(BF16)
        q_ref[0, h, :, QK_NOPE:] = (_rope_pair(y[:, QK_NOPE:], tab) * scale).astype(BF16)


def _q_up(qn, w, tab, batch, seq):
    tm = ROW_TILE
    nseq = seq // tm
    return pl.pallas_call(
        _q_up_kernel,
        grid=(batch, nseq),
        in_specs=[pl.BlockSpec((tm, Q_LORA), lambda b, i: (b * nseq + i, 0)),
                  _resident(w.shape),
                  pl.BlockSpec((tm, 2 * QK_ROPE), lambda b, i: (i, 0))],
        out_specs=pl.BlockSpec((1, H_B, tm, QK_PAD), lambda b, i: (b, 0, i, 0)),
        out_shape=jax.ShapeDtypeStruct((batch, H_B, seq, QK_PAD), BF16),
        compiler_params=_params("arbitrary", "arbitrary"),
        name="q_up",
    )(qn, w, tab)


def _kv_up_kernel(lat_ref, kr_ref, wk_ref, wvt_ref, k_ref, vt_ref):
    lat = lat_ref[...]
    kn = _dot(lat, wk_ref[...]).astype(BF16)
    kr = kr_ref[...]
    for h in range(H_B):
        k_ref[0, h, :, 0:QK_NOPE] = kn[:, h * QK_NOPE:(h + 1) * QK_NOPE]
        k_ref[0, h, :, QK_NOPE:] = kr
    vt = _dot_nt(wvt_ref[...], lat).astype(BF16)
    vt_ref[0, :, 0] = vt.reshape(H_B, V_HEAD, vt.shape[-1])


def _kv_up(lat, kr, wk, wvt, batch, seq):
    tk = ATTN_TK
    nk = seq // tk
    row = lambda b, i: (b * nk + i, 0)
    return pl.pallas_call(
        _kv_up_kernel,
        grid=(batch, nk),
        in_specs=[pl.BlockSpec((tk, KV_LORA), row), pl.BlockSpec((tk, 2 * QK_ROPE), row),
                  _resident(wk.shape), _resident(wvt.shape)],
        out_specs=[pl.BlockSpec((1, H_B, tk, QK_PAD), lambda b, i: (b, 0, i, 0)),
                   pl.BlockSpec((1, H_B, 1, V_HEAD, tk), lambda b, i: (b, 0, i, 0, 0))],
        out_shape=[jax.ShapeDtypeStruct((batch, H_B, seq, QK_PAD), BF16),
                   jax.ShapeDtypeStruct((batch, H_B, nk, V_HEAD, tk), BF16)],
        compiler_params=_params("arbitrary", "arbitrary"),
        name="kv_up",
    )(lat, kr, wk, wvt)


def _attn_kernel(q_ref, k_ref, vt_ref, sg_ref, o_ref, s0_ref, s1_ref, m_ref, l_ref, acc_ref):
    qi = pl.program_id(2)
    q = q_ref[0, 0]
    tk = ATTN_TK

    def scores(j, s_ref):
        kb = k_ref[0, 0, pl.ds(pl.multiple_of(j * tk, tk), tk), :]
        s_ref[...] = _dot_nt(kb, q)

    def update(j, s_ref, masked):
        s = s_ref[...]
        if masked:
            kpos = lax.broadcasted_iota(jnp.int32, s.shape, 0)
            qpos = lax.broadcasted_iota(jnp.int32, s.shape, 1)
            s = jnp.where(kpos <= qpos, s, MASK_VALUE)
        m = m_ref[...]
        m_new = jnp.maximum(m, jnp.max(s, axis=0, keepdims=True))
        alpha = jnp.exp2(m - m_new)
        p = jnp.exp2(s - m_new)
        l_ref[...] = alpha * l_ref[...] + jnp.sum(p, axis=0, keepdims=True)
        acc_ref[...] = alpha * acc_ref[...] + _dot(vt_ref[0, 0, j], p.astype(BF16))
        m_ref[...] = m_new

    m_ref[...] = jnp.full_like(m_ref, MASK_VALUE)
    l_ref[...] = jnp.zeros_like(l_ref)
    acc_ref[...] = jnp.zeros_like(acc_ref)
    scores(0, s0_ref)

    def pair(i, carry):
        j = 2 * i
        scores(j + 1, s1_ref)
        update(j, s0_ref, False)
        scores(j + 2, s0_ref)
        update(j + 1, s1_ref, False)
        return carry

    lax.fori_loop(0, qi // 2, pair, 0)

    @pl.when(qi % 2 == 1)
    def _():
        scores(qi, s1_ref)
        update(qi - 1, s0_ref, False)
        update(qi, s1_ref, True)

    @pl.when(qi % 2 == 0)
    def _():
        update(qi, s0_ref, True)

    o = (acc_ref[...] / l_ref[...]).T
    o_ref[0] = (o * sg_ref[0].astype(F32)).astype(BF16)


def _attention(q, k, vt, sg3):
    b, h, s, _ = q.shape
    tq = ATTN_TQ
    assert tq == ATTN_TK
    nk = s // ATTN_TK
    return pl.pallas_call(
        _attn_kernel,
        grid=(b, h, s // tq),
        in_specs=[pl.BlockSpec((1, 1, tq, QK_PAD), lambda bi, hi, qi: (bi, hi, qi, 0)),
                  pl.BlockSpec((1, 1, s, QK_PAD), lambda bi, hi, qi: (bi, hi, 0, 0)),
                  pl.BlockSpec((1, 1, nk, V_HEAD, ATTN_TK), lambda bi, hi, qi: (bi, hi, 0, 0, 0)),
                  pl.BlockSpec((1, tq, V_HEAD), lambda bi, hi, qi: (bi, qi, hi))],
        out_specs=pl.BlockSpec((1, tq, V_HEAD), lambda bi, hi, qi: (bi, qi, hi)),
        out_shape=jax.ShapeDtypeStruct((b, s, WIDTH_B), BF16),
        scratch_shapes=[pltpu.VMEM((ATTN_TK, tq), F32), pltpu.VMEM((ATTN_TK, tq), F32),
                        pltpu.VMEM((1, tq), F32), pltpu.VMEM((1, tq), F32),
                        pltpu.VMEM((V_HEAD, tq), F32)],
        compiler_params=_params("arbitrary", "arbitrary", "arbitrary"),
        name="mla_attention",
    )(q, k, vt, sg3)


def _rope_angles(seq, half, base):
    inv = base ** (-jnp.arange(half, dtype=F32) / half)
    return jnp.arange(seq, dtype=F32)[:, None] * inv[None, :]


def _retention_decays(c):
    lg = jnp.log1p(-jnp.exp2(-5.0 - jnp.arange(H_A, dtype=F32)))
    idx = jnp.arange(c, dtype=F32)
    diff = idx[:, None] - idx[None, :]
    causal = diff >= 0
    dmat = jnp.where(causal, jnp.exp(jnp.where(causal, diff, 0.0)[None] * lg[:, None, None]), 0.0)
    qdec = jnp.exp((idx + 1.0)[None, :] * lg[:, None])[:, :, None]
    kdec = jnp.exp((c - 1.0 - idx)[None, :] * lg[:, None])[:, :, None]
    cdec = jnp.broadcast_to(jnp.exp(c * lg)[:, None, None], (H_A, 1, DV_A))
    return dmat, qdec, kdec, cdec


def _swap_halves(w):
    half = w.shape[-1] // 2
    return jnp.concatenate([w[..., half:], w[..., :half]], axis=-1)


def kernel(x, a_w_in, a_w_out, b_w_in, b_q_norm, b_w_uq, b_w_out, kv_w_down, kv_norm, kv_w_up,
           ln_g, ln_b):
    batch, seq, _ = x.shape
    t = batch * seq
    x2 = x.reshape(t, D_MODEL)

    ang_a = _rope_angles(seq, DK_A // 2, ROPE_BASE_A)
    cos_a, sin_a = jnp.cos(ang_a), jnp.sin(ang_a)
    ang_b = _rope_angles(seq, QK_ROPE // 2, ROPE_BASE_B)
    cos_b, sin_b = jnp.cos(ang_b), jnp.sin(ang_b)
    tab_b = jnp.concatenate([cos_b, cos_b, -sin_b, sin_b], axis=-1)

    q, k, v, sg = _ret_inproj(x2, a_w_in[0].astype(BF16), cos_a, sin_a, seq)
    shape3 = lambda a: a.reshape(batch, seq, a.shape[-1])
    o = _retention(shape3(q), shape3(k), shape3(v), shape3(sg), *_retention_decays(RET_CHUNK))
    x1 = _outproj_ln(o.reshape(t, WIDTH_A), a_w_out[0].astype(BF16), x2,
                     ln_g[0][None, :], ln_b[0][None, :])

    w_rope = kv_w_down[:, KV_LORA:]
    wdn = jnp.concatenate([kv_w_down[:, :KV_LORA], w_rope, _swap_halves(w_rope)], axis=-1)
    qn, sgb, lat, kr = _mla_inproj(x1, b_w_in[0].astype(BF16), wdn.astype(BF16),
                                   b_q_norm[0][None, :], kv_norm[None, :], tab_b, seq)

    wuq = b_w_uq[0].reshape(Q_LORA, H_B, QK_NOPE + QK_ROPE)
    wuq = jnp.concatenate([wuq, _swap_halves(wuq[..., QK_NOPE:])], axis=-1)
    qh = _q_up(qn, wuq.reshape(Q_LORA, H_B * QK_PAD).astype(BF16), tab_b, batch, seq)

    wup = kv_w_up.reshape(KV_LORA, H_B, QK_NOPE + V_HEAD)
    wk = wup[..., :QK_NOPE].reshape(KV_LORA, H_B * QK_NOPE).astype(BF16)
    wvt = wup[..., QK_NOPE:].reshape(KV_LORA, H_B * V_HEAD).T.astype(BF16)
    kh, vt = _kv_up(lat, kr, wk, wvt, batch, seq)

    ob = _attention(qh, kh, vt, sgb.reshape(batch, seq, WIDTH_B))
    out = _outproj_ln(ob.reshape(t, WIDTH_B), b_w_out[0].astype(BF16), x1,
                      ln_g[1][None, :], ln_b[1][None, :])
    return out.reshape(batch, seq, D_MODEL)
```

```python
import functools
import math

import jax
import jax.numpy as jnp
from jax import lax
from jax.experimental import pallas as pl
from jax.experimental.pallas import tpu as pltpu

D_MODEL = 1024
DEPTH = 2

H_A = 4
DK_A = D_MODEL // H_A
DV_A = 2 * DK_A
WIDTH_A = H_A * DV_A
ROPE_BASE_A = 10000.0

H_B = 16
QK_NOPE = 128
QK_ROPE = 64
V_HEAD = 128
Q_LORA = 768
KV_LORA = 512
WIDTH_B = H_B * V_HEAD
ROPE_BASE_B = 10000.0
QK_PAD = 256

DEEPNORM_ALPHA = (2.0 * DEPTH) ** 0.25

RET_CHUNK = 256
ROW_TILE = 512
ATTN_TQ = 512
ATTN_TK = 512
MASK_VALUE = -1e30

VMEM_LIMIT_BYTES = 56 * 1024 * 1024

BF16 = jnp.bfloat16
F32 = jnp.float32


def _params(*semantics):
    return pltpu.CompilerParams(dimension_semantics=semantics,
                                vmem_limit_bytes=VMEM_LIMIT_BYTES)


def _resident(shape):
    zeros = (0,) * len(shape)
    return pl.BlockSpec(shape, lambda *_: zeros, pipeline_mode=pl.Buffered(1))


def _dot(a, b):
    return jnp.dot(a, b, preferred_element_type=F32)


def _dot_nt(a, b):
    return lax.dot_general(a, b, (((1,), (1,)), ((), ())), preferred_element_type=F32)


def _dot_tn(a, b):
    return lax.dot_general(a, b, (((0,), (0,)), ((), ())), preferred_element_type=F32)


def _silu(g):
    return g / (1.0 + jnp.exp(-g))


def _ret_inproj_kernel(x_ref, w_ref, cos_ref, sin_ref, q_ref, k_ref, v_ref, sg_ref):
    xb = x_ref[...].astype(BF16)
    cos = cos_ref[...]
    sin = sin_ref[...]
    qk = H_A * DK_A
    half = DK_A // 2

    def rope_store(h, out_ref, mult):
        for i in range(H_A):
            x1 = h[:, i * DK_A:i * DK_A + half]
            x2 = h[:, i * DK_A + half:(i + 1) * DK_A]
            out_ref[:, i * DK_A:i * DK_A + half] = ((x1 * cos - x2 * sin) * mult).astype(BF16)
            out_ref[:, i * DK_A + half:(i + 1) * DK_A] = ((x2 * cos + x1 * sin) * mult).astype(BF16)

    rope_store(_dot(xb, w_ref[:, 0:qk]), q_ref, 1.0)
    rope_store(_dot(xb, w_ref[:, qk:2 * qk]), k_ref, DK_A ** -0.5)
    v_ref[...] = _dot(xb, w_ref[:, 2 * qk:2 * qk + WIDTH_A]).astype(BF16)
    sg_ref[...] = _silu(_dot(xb, w_ref[:, 2 * qk + WIDTH_A:])).astype(BF16)


def _ret_inproj(x2, w, cos, sin, seq):
    t = x2.shape[0]
    tm = ROW_TILE
    nseq = seq // tm
    qk = H_A * DK_A
    row = lambda i: (i, 0)
    pos = lambda i: (i % nseq, 0)
    return pl.pallas_call(
        _ret_inproj_kernel,
        grid=(t // tm,),
        in_specs=[pl.BlockSpec((tm, D_MODEL), row),
                  _resident(w.shape),
                  pl.BlockSpec((tm, DK_A // 2), pos),
                  pl.BlockSpec((tm, DK_A // 2), pos)],
        out_specs=[pl.BlockSpec((tm, qk), row), pl.BlockSpec((tm, qk), row),
                   pl.BlockSpec((tm, WIDTH_A), row), pl.BlockSpec((tm, WIDTH_A), row)],
        out_shape=[jax.ShapeDtypeStruct((t, qk), BF16), jax.ShapeDtypeStruct((t, qk), BF16),
                   jax.ShapeDtypeStruct((t, WIDTH_A), BF16), jax.ShapeDtypeStruct((t, WIDTH_A), BF16)],
        compiler_params=_params("arbitrary"),
        name="ret_inproj",
    )(x2, w, cos, sin)


def _retention_kernel(q_ref, k_ref, v_ref, sg_ref, dmat_ref, qdec_ref, kdec_ref, cdec_ref,
                      o_ref, state_ref):
    @pl.when(pl.program_id(2) == 0)
    def _():
        state_ref[...] = jnp.zeros_like(state_ref)

    q = q_ref[0]
    k = k_ref[0]
    v = v_ref[0]
    state = state_ref[...]
    scores = (_dot_nt(q, k) * dmat_ref[0]).astype(BF16)
    qd = (q.astype(F32) * qdec_ref[0]).astype(BF16)
    o = _dot(scores, v) + _dot(qd, state.astype(BF16))
    kd = (k.astype(F32) * kdec_ref[0]).astype(BF16)
    state_ref[...] = state * cdec_ref[0] + _dot_tn(kd, v)

    mu = jnp.mean(o, axis=-1, keepdims=True)
    d = o - mu
    var = jnp.mean(d * d, axis=-1, keepdims=True)
    o_ref[0] = (d * lax.rsqrt(var + 1e-5) * sg_ref[0].astype(F32)).astype(BF16)


def _retention(q, k, v, sg, dmat, qdec, kdec, cdec):
    b, s, _ = q.shape
    c = RET_CHUNK
    tok = lambda bi, h, ci: (bi, ci, h)
    head = lambda bi, h, ci: (h, 0, 0)
    return pl.pallas_call(
        _retention_kernel,
        grid=(b, H_A, s // c),
        in_specs=[pl.BlockSpec((1, c, DK_A), tok), pl.BlockSpec((1, c, DK_A), tok),
                  pl.BlockSpec((1, c, DV_A), tok), pl.BlockSpec((1, c, DV_A), tok),
                  pl.BlockSpec((1, c, c), head), pl.BlockSpec((1, c, 1), head),
                  pl.BlockSpec((1, c, 1), head), pl.BlockSpec((1, 1, DV_A), head)],
        out_specs=pl.BlockSpec((1, c, DV_A), tok),
        out_shape=jax.ShapeDtypeStruct((b, s, WIDTH_A), BF16),
        scratch_shapes=[pltpu.VMEM((DK_A, DV_A), F32)],
        compiler_params=_params("arbitrary", "arbitrary", "arbitrary"),
        name="retention",
    )(q, k, v, sg, dmat, qdec, kdec, cdec)


def _outproj_ln_kernel(a_ref, w_ref, x_ref, g_ref, b_ref, o_ref):
    z = DEEPNORM_ALPHA * x_ref[...] + _dot(a_ref[...], w_ref[...])
    mu = jnp.mean(z, axis=-1, keepdims=True)
    d = z - mu
    var = jnp.mean(d * d, axis=-1, keepdims=True)
    o_ref[...] = d * lax.rsqrt(var + 1e-5) * g_ref[...] + b_ref[...]


def _outproj_ln(a, w, x2, g, bias):
    t, width = a.shape
    tm = ROW_TILE
    row = lambda i: (i, 0)
    return pl.pallas_call(
        _outproj_ln_kernel,
        grid=(t // tm,),
        in_specs=[pl.BlockSpec((tm, width), row), _resident(w.shape),
                  pl.BlockSpec((tm, D_MODEL), row), _resident(g.shape), _resident(bias.shape)],
        out_specs=pl.BlockSpec((tm, D_MODEL), row),
        out_shape=jax.ShapeDtypeStruct((t, D_MODEL), F32),
        compiler_params=_params("arbitrary"),
        name="outproj_ln",
    )(a, w, x2, g, bias)


def _rope_pair(u, tab):
    t = u * tab
    lane = lax.broadcasted_iota(jnp.int32, t.shape, 1)
    return jnp.where(lane < QK_ROPE, t + pltpu.roll(t, QK_ROPE, 1), 0.0)


def _rms(x, g, eps=1e-6):
    return x * lax.rsqrt(jnp.mean(x * x, axis=-1, keepdims=True) + eps) * g


def _mla_inproj_kernel(x_ref, win_ref, wdn_ref, qg_ref, kvg_ref, tab_ref,
                       qn_ref, sg_ref, lat_ref, kr_ref):
    xb = x_ref[...].astype(BF16)
    qn_ref[...] = _rms(_dot(xb, win_ref[:, 0:Q_LORA]), qg_ref[...]).astype(BF16)
    sg_ref[...] = _silu(_dot(xb, win_ref[:, Q_LORA:])).astype(BF16)
    c = _dot(xb, wdn_ref[...])
    lat_ref[...] = _rms(c[:, 0:KV_LORA], kvg_ref[...]).astype(BF16)
    kr_ref[...] = _rope_pair(c[:, KV_LORA:], tab_ref[...]).astype(BF16)


def _mla_inproj(x2, win, wdn, qg, kvg, tab, seq):
    t = x2.shape[0]
    tm = ROW_TILE
    nseq = seq // tm
    row = lambda i: (i, 0)
    pos = lambda i: (i % nseq, 0)
    widths = (Q_LORA, WIDTH_B, KV_LORA, 2 * QK_ROPE)
    return pl.pallas_call(
        _mla_inproj_kernel,
        grid=(t // tm,),
        in_specs=[pl.BlockSpec((tm, D_MODEL), row), _resident(win.shape), _resident(wdn.shape),
                  _resident(qg.shape), _resident(kvg.shape),
                  pl.BlockSpec((tm, 2 * QK_ROPE), pos)],
        out_specs=[pl.BlockSpec((tm, w), row) for w in widths],
        out_shape=[jax.ShapeDtypeStruct((t, w), BF16) for w in widths],
        compiler_params=_params("arbitrary"),
        name="mla_inproj",
    )(x2, win, wdn, qg, kvg, tab)


def _q_up_kernel(qn_ref, w_ref, tab_ref, q_ref):
    qn = qn_ref[...]
    tab = tab_ref[...]
    scale = (QK_NOPE + QK_ROPE) ** -0.5 * math.log2(math.e)
    for h in range(H_B):
        y = _dot(qn, w_ref[:, h * QK_PAD:(h + 1) * QK_PAD])
        q_ref[0, h, :, 0:QK_NOPE] = (y[:, 0:QK_NOPE] * scale).astype(BF16)
        q_ref[0, h, :, QK_NOPE:] = (_rope_pair(y[:, QK_NOPE:], tab) * scale).astype(BF16)


def _q_up(qn, w, tab, batch, seq):
    tm = ROW_TILE
    nseq = seq // tm
    return pl.pallas_call(
        _q_up_kernel,
        grid=(batch, nseq),
        in_specs=[pl.BlockSpec((tm, Q_LORA), lambda b, i: (b * nseq + i, 0)),
                  _resident(w.shape),
                  pl.BlockSpec((tm, 2 * QK_ROPE), lambda b, i: (i, 0))],
        out_specs=pl.BlockSpec((1, H_B, tm, QK_PAD), lambda b, i: (b, 0, i, 0)),
        out_shape=jax.ShapeDtypeStruct((batch, H_B, seq, QK_PAD), BF16),
        compiler_params=_params("arbitrary", "arbitrary"),
        name="q_up",
    )(qn, w, tab)


def _kv_up_kernel(lat_ref, kr_ref, wk_ref, wvt_ref, k_ref, vt_ref):
    lat = lat_ref[...]
    kn = _dot(lat, wk_ref[...]).astype(BF16)
    kr = kr_ref[...]
    for h in range(H_B):
        k_ref[0, h, :, 0:QK_NOPE] = kn[:, h * QK_NOPE:(h + 1) * QK_NOPE]
        k_ref[0, h, :, QK_NOPE:] = kr
    vt = _dot_nt(wvt_ref[...], lat).astype(BF16)
    vt_ref[0, :, 0] = vt.reshape(H_B, V_HEAD, vt.shape[-1])


def _kv_up(lat, kr, wk, wvt, batch, seq):
    tk = ATTN_TK
    nk = seq // tk
    row = lambda b, i: (b * nk + i, 0)
    return pl.pallas_call(
        _kv_up_kernel,
        grid=(batch, nk),
        in_specs=[pl.BlockSpec((tk, KV_LORA), row), pl.BlockSpec((tk, 2 * QK_ROPE), row),
                  _resident(wk.shape), _resident(wvt.shape)],
        out_specs=[pl.BlockSpec((1, H_B, tk, QK_PAD), lambda b, i: (b, 0, i, 0)),
                   pl.BlockSpec((1, H_B, 1, V_HEAD, tk), lambda b, i: (b, 0, i, 0, 0))],
        out_shape=[jax.ShapeDtypeStruct((batch, H_B, seq, QK_PAD), BF16),
                   jax.ShapeDtypeStruct((batch, H_B, nk, V_HEAD, tk), BF16)],
        compiler_params=_params("arbitrary", "arbitrary"),
        name="kv_up",
    )(lat, kr, wk, wvt)


def _attn_kernel(q_ref, k_ref, vt_ref, sg_ref, o_ref, sa_ref, sb_ref, m_ref, l_ref, acc_ref):
    r = pl.program_id(2)
    tq = ATTN_TQ
    tk = ATTN_TK
    q_rows = (q_ref[0, 0, 0:tq, :], q_ref[0, 0, tq:2 * tq, :])

    def scores(row, j, nblk, s_ref, off=0):
        kb = k_ref[0, 0, pl.ds(pl.multiple_of(j * tk, tk), nblk * tk), :]
        s_ref[off:off + nblk * tk, :] = _dot_nt(kb, q_rows[row])

    def update(row, j, nblk, s_ref, masked, off=0):
        s = s_ref[off:off + nblk * tk, :]
        if masked:
            kpos = lax.broadcasted_iota(jnp.int32, s.shape, 0)
            qpos = lax.broadcasted_iota(jnp.int32, s.shape, 1)
            s = jnp.where(kpos <= qpos, s, MASK_VALUE)
        m = m_ref[row]
        m_new = jnp.maximum(m, jnp.max(s, axis=0, keepdims=True))
        alpha = jnp.exp2(m - m_new)
        p = jnp.exp2(s - m_new)
        l_ref[row] = alpha * l_ref[row] + jnp.sum(p, axis=0, keepdims=True)
        m_ref[row] = m_new
        p = p.astype(BF16)
        pv = _dot(vt_ref[0, 0, j], p[0:tk])
        for i in range(1, nblk):
            pv += _dot(vt_ref[0, 0, j + i], p[i * tk:(i + 1) * tk])
        acc_ref[row] = alpha * acc_ref[row] + pv

    def finalize(row):
        o = (acc_ref[row] / l_ref[row]).T
        rows = slice(row * tq, (row + 1) * tq)
        o_ref[0, rows, :] = (o * sg_ref[0, rows, :].astype(F32)).astype(BF16)

    m_ref[...] = jnp.full_like(m_ref, MASK_VALUE)
    l_ref[...] = jnp.zeros_like(l_ref)
    acc_ref[...] = jnp.zeros_like(acc_ref)

    @pl.when(r == 0)
    def _():
        scores(0, 0, 1, sa_ref, off=tk)

    @pl.when(r > 0)
    def _():
        scores(0, 0, 2, sa_ref)

        def body(i, carry):
            j = 2 * i
            scores(1, j, 2, sb_ref)
            update(0, j, 2, sa_ref, False)
            scores(0, jnp.minimum(j + 2, 2 * r - 1), 2, sa_ref)
            update(1, j, 2, sb_ref, False)
            return carry

        lax.fori_loop(0, r, body, 0)

    scores(1, 2 * r, 1, sb_ref)
    update(0, 2 * r, 1, sa_ref, True, off=tk)
    scores(1, 2 * r + 1, 1, sa_ref)
    update(1, 2 * r, 1, sb_ref, False)
    finalize(0)
    update(1, 2 * r + 1, 1, sa_ref, True)
    finalize(1)


def _attention(q, k, vt, sg3):
    b, h, s, _ = q.shape
    tq = ATTN_TQ
    assert tq == ATTN_TK
    nk = s // ATTN_TK
    qblk = lambda bi, hi, ri: (bi, hi, ri, 0)
    oblk = lambda bi, hi, ri: (bi, ri, hi)
    return pl.pallas_call(
        _attn_kernel,
        grid=(b, h, s // (2 * tq)),
        in_specs=[pl.BlockSpec((1, 1, 2 * tq, QK_PAD), qblk),
                  pl.BlockSpec((1, 1, s, QK_PAD), lambda bi, hi, ri: (bi, hi, 0, 0)),
                  pl.BlockSpec((1, 1, nk, V_HEAD, ATTN_TK), lambda bi, hi, ri: (bi, hi, 0, 0, 0)),
                  pl.BlockSpec((1, 2 * tq, V_HEAD), oblk)],
        out_specs=pl.BlockSpec((1, 2 * tq, V_HEAD), oblk),
        out_shape=jax.ShapeDtypeStruct((b, s, WIDTH_B), BF16),
        scratch_shapes=[pltpu.VMEM((2 * ATTN_TK, tq), F32), pltpu.VMEM((2 * ATTN_TK, tq), F32),
                        pltpu.VMEM((2, 1, tq), F32), pltpu.VMEM((2, 1, tq), F32),
                        pltpu.VMEM((2, V_HEAD, tq), F32)],
        compiler_params=_params("arbitrary", "arbitrary", "arbitrary"),
        name="mla_attention",
    )(q, k, vt, sg3)


def _rope_angles(seq, half, base):
    inv = base ** (-jnp.arange(half, dtype=F32) / half)
    return jnp.arange(seq, dtype=F32)[:, None] * inv[None, :]


def _retention_decays(c):
    lg = jnp.log1p(-jnp.exp2(-5.0 - jnp.arange(H_A, dtype=F32)))
    idx = jnp.arange(c, dtype=F32)
    diff = idx[:, None] - idx[None, :]
    causal = diff >= 0
    dmat = jnp.where(causal, jnp.exp(jnp.where(causal, diff, 0.0)[None] * lg[:, None, None]), 0.0)
    qdec = jnp.exp((idx + 1.0)[None, :] * lg[:, None])[:, :, None]
    kdec = jnp.exp((c - 1.0 - idx)[None, :] * lg[:, None])[:, :, None]
    cdec = jnp.broadcast_to(jnp.exp(c * lg)[:, None, None], (H_A, 1, DV_A))
    return dmat, qdec, kdec, cdec


def _swap_halves(w):
    half = w.shape[-1] // 2
    return jnp.concatenate([w[..., half:], w[..., :half]], axis=-1)


def kernel(x, a_w_in, a_w_out, b_w_in, b_q_norm, b_w_uq, b_w_out, kv_w_down, kv_norm, kv_w_up,
           ln_g, ln_b):
    batch, seq, _ = x.shape
    t = batch * seq
    x2 = x.reshape(t, D_MODEL)

    ang_a = _rope_angles(seq, DK_A // 2, ROPE_BASE_A)
    cos_a, sin_a = jnp.cos(ang_a), jnp.sin(ang_a)
    ang_b = _rope_angles(seq, QK_ROPE // 2, ROPE_BASE_B)
    cos_b, sin_b = jnp.cos(ang_b), jnp.sin(ang_b)
    tab_b = jnp.concatenate([cos_b, cos_b, -sin_b, sin_b], axis=-1)

    q, k, v, sg = _ret_inproj(x2, a_w_in[0].astype(BF16), cos_a, sin_a, seq)
    shape3 = lambda a: a.reshape(batch, seq, a.shape[-1])
    o = _retention(shape3(q), shape3(k), shape3(v), shape3(sg), *_retention_decays(RET_CHUNK))
    x1 = _outproj_ln(o.reshape(t, WIDTH_A), a_w_out[0].astype(BF16), x2,
                     ln_g[0][None, :], ln_b[0][None, :])

    w_rope = kv_w_down[:, KV_LORA:]
    wdn = jnp.concatenate([kv_w_down[:, :KV_LORA], w_rope, _swap_halves(w_rope)], axis=-1)
    qn, sgb, lat, kr = _mla_inproj(x1, b_w_in[0].astype(BF16), wdn.astype(BF16),
                                   b_q_norm[0][None, :], kv_norm[None, :], tab_b, seq)

    wuq = b_w_uq[0].reshape(Q_LORA, H_B, QK_NOPE + QK_ROPE)
    wuq = jnp.concatenate([wuq, _swap_halves(wuq[..., QK_NOPE:])], axis=-1)
    qh = _q_up(qn, wuq.reshape(Q_LORA, H_B * QK_PAD).astype(BF16), tab_b, batch, seq)

    wup = kv_w_up.reshape(KV_LORA, H_B, QK_NOPE + V_HEAD)
    wk = wup[..., :QK_NOPE].reshape(KV_LORA, H_B * QK_NOPE).astype(BF16)
    wvt = wup[..., QK_NOPE:].reshape(KV_LORA, H_B * V_HEAD).T.astype(BF16)
    kh, vt = _kv_up(lat, kr, wk, wvt, batch, seq)

    ob = _attention(qh, kh, vt, sgb.reshape(batch, seq, WIDTH_B))
    out = _outproj_ln(ob.reshape(t, WIDTH_B), b_w_out[0].astype(BF16), x1,
                      ln_g[1][None, :], ln_b[1][None, :])
    return out.reshape(batch, seq, D_MODEL)
```

```python
import functools
import math

import jax
import jax.numpy as jnp
from jax import lax
from jax.experimental import pallas as pl
from jax.experimental.pallas import tpu as pltpu

D_MODEL = 1024
DEPTH = 2

H_A = 4
DK_A = D_MODEL // H_A
DV_A = 2 * DK_A
WIDTH_A = H_A * DV_A
ROPE_BASE_A = 10000.0

H_B = 16
QK_NOPE = 128
QK_ROPE = 64
V_HEAD = 128
Q_LORA = 768
KV_LORA = 512
WIDTH_B = H_B * V_HEAD
ROPE_BASE_B = 10000.0
QK_PAD = 256

DEEPNORM_ALPHA = (2.0 * DEPTH) ** 0.25

RET_CHUNK = 256
ROW_TILE = 512
ATTN_TQ = 512
ATTN_TK = 512
MASK_VALUE = -1e30

VMEM_LIMIT_BYTES = 56 * 1024 * 1024

BF16 = jnp.bfloat16
F32 = jnp.float32


def _params(*semantics):
    return pltpu.CompilerParams(dimension_semantics=semantics,
                                vmem_limit_bytes=VMEM_LIMIT_BYTES)


def _resident(shape):
    zeros = (0,) * len(shape)
    return pl.BlockSpec(shape, lambda *_: zeros, pipeline_mode=pl.Buffered(1))


def _dot(a, b):
    return jnp.dot(a, b, preferred_element_type=F32)


def _dot_nt(a, b):
    return lax.dot_general(a, b, (((1,), (1,)), ((), ())), preferred_element_type=F32)


def _dot_tn(a, b):
    return lax.dot_general(a, b, (((0,), (0,)), ((), ())), preferred_element_type=F32)


def _silu(g):
    return g / (1.0 + jnp.exp(-g))


def _ret_inproj_kernel(x_ref, w_ref, cos_ref, sin_ref, q_ref, k_ref, v_ref, sg_ref):
    xb = x_ref[...].astype(BF16)
    cos = cos_ref[...]
    sin = sin_ref[...]
    qk = H_A * DK_A
    half = DK_A // 2

    def rope_store(h, out_ref, mult):
        for i in range(H_A):
            x1 = h[:, i * DK_A:i * DK_A + half]
            x2 = h[:, i * DK_A + half:(i + 1) * DK_A]
            out_ref[:, i * DK_A:i * DK_A + half] = ((x1 * cos - x2 * sin) * mult).astype(BF16)
            out_ref[:, i * DK_A + half:(i + 1) * DK_A] = ((x2 * cos + x1 * sin) * mult).astype(BF16)

    rope_store(_dot(xb, w_ref[:, 0:qk]), q_ref, 1.0)
    rope_store(_dot(xb, w_ref[:, qk:2 * qk]), k_ref, DK_A ** -0.5)
    v_ref[...] = _dot(xb, w_ref[:, 2 * qk:2 * qk + WIDTH_A]).astype(BF16)
    sg_ref[...] = _silu(_dot(xb, w_ref[:, 2 * qk + WIDTH_A:])).astype(BF16)


def _ret_inproj(x2, w, cos, sin, seq):
    t = x2.shape[0]
    tm = ROW_TILE
    nseq = seq // tm
    qk = H_A * DK_A
    row = lambda i: (i, 0)
    pos = lambda i: (i % nseq, 0)
    return pl.pallas_call(
        _ret_inproj_kernel,
        grid=(t // tm,),
        in_specs=[pl.BlockSpec((tm, D_MODEL), row),
                  _resident(w.shape),
                  pl.BlockSpec((tm, DK_A // 2), pos),
                  pl.BlockSpec((tm, DK_A // 2), pos)],
        out_specs=[pl.BlockSpec((tm, qk), row), pl.BlockSpec((tm, qk), row),
                   pl.BlockSpec((tm, WIDTH_A), row), pl.BlockSpec((tm, WIDTH_A), row)],
        out_shape=[jax.ShapeDtypeStruct((t, qk), BF16), jax.ShapeDtypeStruct((t, qk), BF16),
                   jax.ShapeDtypeStruct((t, WIDTH_A), BF16), jax.ShapeDtypeStruct((t, WIDTH_A), BF16)],
        compiler_params=_params("arbitrary"),
        name="ret_inproj",
    )(x2, w, cos, sin)


def _retention_kernel(q_ref, k_ref, v_ref, sg_ref, dmat_ref, qdec_ref, kdec_ref, cdec_ref,
                      o_ref, state_ref):
    @pl.when(pl.program_id(2) == 0)
    def _():
        state_ref[...] = jnp.zeros_like(state_ref)

    q = q_ref[0]
    k = k_ref[0]
    v = v_ref[0]
    state = state_ref[...]
    scores = (_dot_nt(q, k) * dmat_ref[0]).astype(BF16)
    qd = (q.astype(F32) * qdec_ref[0]).astype(BF16)
    o = _dot(scores, v) + _dot(qd, state.astype(BF16))
    kd = (k.astype(F32) * kdec_ref[0]).astype(BF16)
    state_ref[...] = state * cdec_ref[0] + _dot_tn(kd, v)

    mu = jnp.mean(o, axis=-1, keepdims=True)
    d = o - mu
    var = jnp.mean(d * d, axis=-1, keepdims=True)
    o_ref[0] = (d * lax.rsqrt(var + 1e-5) * sg_ref[0].astype(F32)).astype(BF16)


def _retention(q, k, v, sg, dmat, qdec, kdec, cdec):
    b, s, _ = q.shape
    c = RET_CHUNK
    tok = lambda bi, h, ci: (bi, ci, h)
    head = lambda bi, h, ci: (h, 0, 0)
    return pl.pallas_call(
        _retention_kernel,
        grid=(b, H_A, s // c),
        in_specs=[pl.BlockSpec((1, c, DK_A), tok), pl.BlockSpec((1, c, DK_A), tok),
                  pl.BlockSpec((1, c, DV_A), tok), pl.BlockSpec((1, c, DV_A), tok),
                  pl.BlockSpec((1, c, c), head), pl.BlockSpec((1, c, 1), head),
                  pl.BlockSpec((1, c, 1), head), pl.BlockSpec((1, 1, DV_A), head)],
        out_specs=pl.BlockSpec((1, c, DV_A), tok),
        out_shape=jax.ShapeDtypeStruct((b, s, WIDTH_A), BF16),
        scratch_shapes=[pltpu.VMEM((DK_A, DV_A), F32)],
        compiler_params=_params("arbitrary", "arbitrary", "arbitrary"),
        name="retention",
    )(q, k, v, sg, dmat, qdec, kdec, cdec)


def _outproj_ln_kernel(a_ref, w_ref, x_ref, g_ref, b_ref, o_ref):
    z = DEEPNORM_ALPHA * x_ref[...] + _dot(a_ref[...], w_ref[...])
    mu = jnp.mean(z, axis=-1, keepdims=True)
    d = z - mu
    var = jnp.mean(d * d, axis=-1, keepdims=True)
    o_ref[...] = d * lax.rsqrt(var + 1e-5) * g_ref[...] + b_ref[...]


def _outproj_ln(a, w, x2, g, bias):
    t, width = a.shape
    tm = ROW_TILE
    row = lambda i: (i, 0)
    return pl.pallas_call(
        _outproj_ln_kernel,
        grid=(t // tm,),
        in_specs=[pl.BlockSpec((tm, width), row), _resident(w.shape),
                  pl.BlockSpec((tm, D_MODEL), row), _resident(g.shape), _resident(bias.shape)],
        out_specs=pl.BlockSpec((tm, D_MODEL), row),
        out_shape=jax.ShapeDtypeStruct((t, D_MODEL), F32),
        compiler_params=_params("arbitrary"),
        name="outproj_ln",
    )(a, w, x2, g, bias)


def _rope_pair(u, tab):
    t = u * tab
    lane = lax.broadcasted_iota(jnp.int32, t.shape, 1)
    return jnp.where(lane < QK_ROPE, t + pltpu.roll(t, QK_ROPE, 1), 0.0)


def _rms(x, g, eps=1e-6):
    return x * lax.rsqrt(jnp.mean(x * x, axis=-1, keepdims=True) + eps) * g


def _mla_inproj_kernel(x_ref, win_ref, wdn_ref, qg_ref, kvg_ref, tab_ref,
                       qn_ref, sg_ref, lat_ref, kr_ref):
    xb = x_ref[...].astype(BF16)
    qn_ref[...] = _rms(_dot(xb, win_ref[:, 0:Q_LORA]), qg_ref[...]).astype(BF16)
    sg_ref[...] = _silu(_dot(xb, win_ref[:, Q_LORA:])).astype(BF16)
    c = _dot(xb, wdn_ref[...])
    lat_ref[...] = _rms(c[:, 0:KV_LORA], kvg_ref[...]).astype(BF16)
    kr_ref[...] = _rope_pair(c[:, KV_LORA:], tab_ref[...]).astype(BF16)


def _mla_inproj(x2, win, wdn, qg, kvg, tab, seq):
    t = x2.shape[0]
    tm = ROW_TILE
    nseq = seq // tm
    row = lambda i: (i, 0)
    pos = lambda i: (i % nseq, 0)
    widths = (Q_LORA, WIDTH_B, KV_LORA, 2 * QK_ROPE)
    return pl.pallas_call(
        _mla_inproj_kernel,
        grid=(t // tm,),
        in_specs=[pl.BlockSpec((tm, D_MODEL), row), _resident(win.shape), _resident(wdn.shape),
                  _resident(qg.shape), _resident(kvg.shape),
                  pl.BlockSpec((tm, 2 * QK_ROPE), pos)],
        out_specs=[pl.BlockSpec((tm, w), row) for w in widths],
        out_shape=[jax.ShapeDtypeStruct((t, w), BF16) for w in widths],
        compiler_params=_params("arbitrary"),
        name="mla_inproj",
    )(x2, win, wdn, qg, kvg, tab)


def _q_up_kernel(qn_ref, w_ref, tab_ref, q_ref):
    qn = qn_ref[...]
    tab = tab_ref[...]
    scale = (QK_NOPE + QK_ROPE) ** -0.5 * math.log2(math.e)
    for h in range(H_B):
        y = _dot(qn, w_ref[:, h * QK_PAD:(h + 1) * QK_PAD])
        q_ref[0, h, :, 0:QK_NOPE] = (y[:, 0:QK_NOPE] * scale).astype(BF16)
        q_ref[0, h, :, QK_NOPE:] = (_rope_pair(y[:, QK_NOPE:], tab) * scale).astype(BF16)


def _q_up(qn, w, tab, batch, seq):
    tm = ROW_TILE
    nseq = seq // tm
    return pl.pallas_call(
        _q_up_kernel,
        grid=(batch, nseq),
        in_specs=[pl.BlockSpec((tm, Q_LORA), lambda b, i: (b * nseq + i, 0)),
                  _resident(w.shape),
                  pl.BlockSpec((tm, 2 * QK_ROPE), lambda b, i: (i, 0))],
        out_specs=pl.BlockSpec((1, H_B, tm, QK_PAD), lambda b, i: (b, 0, i, 0)),
        out_shape=jax.ShapeDtypeStruct((batch, H_B, seq, QK_PAD), BF16),
        compiler_params=_params("arbitrary", "arbitrary"),
        name="q_up",
    )(qn, w, tab)


def _kv_up_kernel(lat_ref, kr_ref, wk_ref, wvt_ref, k_ref, vt_ref):
    lat = lat_ref[...]
    kn = _dot(lat, wk_ref[...]).astype(BF16)
    kr = kr_ref[...]
    for h in range(H_B):
        k_ref[0, h, :, 0:QK_NOPE] = kn[:, h * QK_NOPE:(h + 1) * QK_NOPE]
        k_ref[0, h, :, QK_NOPE:] = kr
    vt = _dot_nt(wvt_ref[...], lat).astype(BF16)
    vt_ref[0, :, 0] = vt.reshape(H_B, V_HEAD, vt.shape[-1])


def _kv_up(lat, kr, wk, wvt, batch, seq):
    tk = ATTN_TK
    nk = seq // tk
    row = lambda b, i: (b * nk + i, 0)
    return pl.pallas_call(
        _kv_up_kernel,
        grid=(batch, nk),
        in_specs=[pl.BlockSpec((tk, KV_LORA), row), pl.BlockSpec((tk, 2 * QK_ROPE), row),
                  _resident(wk.shape), _resident(wvt.shape)],
        out_specs=[pl.BlockSpec((1, H_B, tk, QK_PAD), lambda b, i: (b, 0, i, 0)),
                   pl.BlockSpec((1, H_B, 1, V_HEAD, tk), lambda b, i: (b, 0, i, 0, 0))],
        out_shape=[jax.ShapeDtypeStruct((batch, H_B, seq, QK_PAD), BF16),
                   jax.ShapeDtypeStruct((batch, H_B, nk, V_HEAD, tk), BF16)],
        compiler_params=_params("arbitrary", "arbitrary"),
        name="kv_up",
    )(lat, kr, wk, wvt)


def _attn_kernel(q_ref, k_ref, vt_ref, sg_ref, o_ref, sa_ref, sb_ref, m_ref, l_ref, acc_ref):
    r = pl.program_id(2)
    last = pl.num_programs(2) - 1
    tq = ATTN_TQ
    tk = ATTN_TK

    def q_rows(pair, row):
        start = pl.multiple_of((2 * pair + row) * tq, tq)
        return q_ref[0, 0, pl.ds(start, tq), :]

    def scores(q, j, nblk, s_ref, off=0):
        kb = k_ref[0, 0, pl.ds(pl.multiple_of(j * tk, tk), nblk * tk), :]
        s_ref[off:off + nblk * tk, :] = _dot_nt(kb, q)

    def update(row, j, nblk, s_ref, diag_at=None, off=0):
        s = s_ref[off:off + nblk * tk, :]
        if diag_at is not None:
            kpos = lax.broadcasted_iota(jnp.int32, s.shape, 0)
            qpos = lax.broadcasted_iota(jnp.int32, s.shape, 1)
            s = jnp.where(kpos <= qpos + diag_at, s, MASK_VALUE)
        m = m_ref[row]
        m_new = jnp.maximum(m, jnp.max(s, axis=0, keepdims=True))
        alpha = jnp.exp2(m - m_new)
        p = jnp.exp2(s - m_new)
        l_ref[row] = alpha * l_ref[row] + jnp.sum(p, axis=0, keepdims=True)
        m_ref[row] = m_new
        p = p.astype(BF16)
        pv = _dot(vt_ref[0, 0, j], p[0:tk])
        for i in range(1, nblk):
            pv += _dot(vt_ref[0, 0, j + i], p[i * tk:(i + 1) * tk])
        acc_ref[row] = alpha * acc_ref[row] + pv

    def finalize(row):
        o = (acc_ref[row] / l_ref[row]).T
        rows = slice(row * tq, (row + 1) * tq)
        o_ref[0, rows, :] = (o * sg_ref[0, rows, :].astype(F32)).astype(BF16)

    m_ref[...] = jnp.full_like(m_ref, MASK_VALUE)
    l_ref[...] = jnp.zeros_like(l_ref)
    acc_ref[...] = jnp.zeros_like(acc_ref)

    @pl.when(r == 0)
    def _():
        scores(q_rows(r, 0), 0, 1, sa_ref, off=tk)

    @pl.when(r > 0)
    def _():
        def step(t):
            j = 2 * t
            scores(q_rows(r, 1), j, 2, sb_ref)
            update(0, j, 2, sa_ref)
            scores(q_rows(r, 0), jnp.minimum(j + 2, 2 * r - 1), 2, sa_ref)
            update(1, j, 2, sb_ref)

        def body(i, carry):
            step(2 * i)
            step(2 * i + 1)
            return carry

        lax.fori_loop(0, r // 2, body, 0)

        @pl.when(r % 2 == 1)
        def _():
            step(r - 1)

    scores(q_rows(r, 1), 2 * r, 2, sb_ref)
    update(0, 2 * r, 1, sa_ref, diag_at=0, off=tk)
    scores(q_rows(jnp.minimum(r + 1, last), 0), 0, 2, sa_ref)
    update(1, 2 * r, 2, sb_ref, diag_at=tk)
    finalize(0)
    finalize(1)


def _attention(q, k, vt, sg3):
    b, h, s, _ = q.shape
    tq = ATTN_TQ
    assert tq == ATTN_TK
    nk = s // ATTN_TK
    head = lambda bi, hi, ri: (bi, hi, 0, 0)
    oblk = lambda bi, hi, ri: (bi, ri, hi)
    return pl.pallas_call(
        _attn_kernel,
        grid=(b, h, s // (2 * tq)),
        in_specs=[pl.BlockSpec((1, 1, s, QK_PAD), head),
                  pl.BlockSpec((1, 1, s, QK_PAD), head),
                  pl.BlockSpec((1, 1, nk, V_HEAD, ATTN_TK), lambda bi, hi, ri: (bi, hi, 0, 0, 0)),
                  pl.BlockSpec((1, 2 * tq, V_HEAD), oblk)],
        out_specs=pl.BlockSpec((1, 2 * tq, V_HEAD), oblk),
        out_shape=jax.ShapeDtypeStruct((b, s, WIDTH_B), BF16),
        scratch_shapes=[pltpu.VMEM((2 * ATTN_TK, tq), F32), pltpu.VMEM((2 * ATTN_TK, tq), F32),
                        pltpu.VMEM((2, 1, tq), F32), pltpu.VMEM((2, 1, tq), F32),
                        pltpu.VMEM((2, V_HEAD, tq), F32)],
        compiler_params=_params("arbitrary", "arbitrary", "arbitrary"),
        name="mla_attention",
    )(q, k, vt, sg3)


def _rope_angles(seq, half, base):
    inv = base ** (-jnp.arange(half, dtype=F32) / half)
    return jnp.arange(seq, dtype=F32)[:, None] * inv[None, :]


def _retention_decays(c):
    lg = jnp.log1p(-jnp.exp2(-5.0 - jnp.arange(H_A, dtype=F32)))
    idx = jnp.arange(c, dtype=F32)
    diff = idx[:, None] - idx[None, :]
    causal = diff >= 0
    dmat = jnp.where(causal, jnp.exp(jnp.where(causal, diff, 0.0)[None] * lg[:, None, None]), 0.0)
    qdec = jnp.exp((idx + 1.0)[None, :] * lg[:, None])[:, :, None]
    kdec = jnp.exp((c - 1.0 - idx)[None, :] * lg[:, None])[:, :, None]
    cdec = jnp.broadcast_to(jnp.exp(c * lg)[:, None, None], (H_A, 1, DV_A))
    return dmat, qdec, kdec, cdec


def _swap_halves(w):
    half = w.shape[-1] // 2
    return jnp.concatenate([w[..., half:], w[..., :half]], axis=-1)


def kernel(x, a_w_in, a_w_out, b_w_in, b_q_norm, b_w_uq, b_w_out, kv_w_down, kv_norm, kv_w_up,
           ln_g, ln_b):
    batch, seq, _ = x.shape
    t = batch * seq
    x2 = x.reshape(t, D_MODEL)

    ang_a = _rope_angles(seq, DK_A // 2, ROPE_BASE_A)
    cos_a, sin_a = jnp.cos(ang_a), jnp.sin(ang_a)
    ang_b = _rope_angles(seq, QK_ROPE // 2, ROPE_BASE_B)
    cos_b, sin_b = jnp.cos(ang_b), jnp.sin(ang_b)
    tab_b = jnp.concatenate([cos_b, cos_b, -sin_b, sin_b], axis=-1)

    q, k, v, sg = _ret_inproj(x2, a_w_in[0].astype(BF16), cos_a, sin_a, seq)
    shape3 = lambda a: a.reshape(batch, seq, a.shape[-1])
    o = _retention(shape3(q), shape3(k), shape3(v), shape3(sg), *_retention_decays(RET_CHUNK))
    x1 = _outproj_ln(o.reshape(t, WIDTH_A), a_w_out[0].astype(BF16), x2,
                     ln_g[0][None, :], ln_b[0][None, :])

    w_rope = kv_w_down[:, KV_LORA:]
    wdn = jnp.concatenate([kv_w_down[:, :KV_LORA], w_rope, _swap_halves(w_rope)], axis=-1)
    qn, sgb, lat, kr = _mla_inproj(x1, b_w_in[0].astype(BF16), wdn.astype(BF16),
                                   b_q_norm[0][None, :], kv_norm[None, :], tab_b, seq)

    wuq = b_w_uq[0].reshape(Q_LORA, H_B, QK_NOPE + QK_ROPE)
    wuq = jnp.concatenate([wuq, _swap_halves(wuq[..., QK_NOPE:])], axis=-1)
    qh = _q_up(qn, wuq.reshape(Q_LORA, H_B * QK_PAD).astype(BF16), tab_b, batch, seq)

    wup = kv_w_up.reshape(KV_LORA, H_B, QK_NOPE + V_HEAD)
    wk = wup[..., :QK_NOPE].reshape(KV_LORA, H_B * QK_NOPE).astype(BF16)
    wvt = wup[..., QK_NOPE:].reshape(KV_LORA, H_B * V_HEAD).T.astype(BF16)
    kh, vt = _kv_up(lat, kr, wk, wvt, batch, seq)

    ob = _attention(qh, kh, vt, sgb.reshape(batch, seq, WIDTH_B))
    out = _outproj_ln(ob.reshape(t, WIDTH_B), b_w_out[0].astype(BF16), x1,
                      ln_g[1][None, :], ln_b[1][None, :])
    return out.reshape(batch, seq, D_MODEL)
```

```python
import functools
import math

import jax
import jax.numpy as jnp
from jax import lax
from jax.experimental import pallas as pl
from jax.experimental.pallas import tpu as pltpu

D_MODEL = 1024
DEPTH = 2

H_A = 4
DK_A = D_MODEL // H_A
DV_A = 2 * DK_A
WIDTH_A = H_A * DV_A
ROPE_BASE_A = 10000.0

H_B = 16
QK_NOPE = 128
QK_ROPE = 64
V_HEAD = 128
Q_LORA = 768
KV_LORA = 512
WIDTH_B = H_B * V_HEAD
ROPE_BASE_B = 10000.0
QK_PAD = 256

DEEPNORM_ALPHA = (2.0 * DEPTH) ** 0.25

RET_CHUNK = 256
ROW_TILE = 512
WIDE_ROW_TILE = 1024
ROW_SLAB = 256
ATTN_TQ = 512
ATTN_TK = 512
MASK_VALUE = -1e30

VMEM_LIMIT_BYTES = 56 * 1024 * 1024

BF16 = jnp.bfloat16
F32 = jnp.float32


def _params(*semantics):
    return pltpu.CompilerParams(dimension_semantics=semantics,
                                vmem_limit_bytes=VMEM_LIMIT_BYTES)


def _resident(shape):
    zeros = (0,) * len(shape)
    return pl.BlockSpec(shape, lambda *_: zeros, pipeline_mode=pl.Buffered(1))


def _dot(a, b):
    return jnp.dot(a, b, preferred_element_type=F32)


def _dot_nt(a, b):
    return lax.dot_general(a, b, (((1,), (1,)), ((), ())), preferred_element_type=F32)


def _dot_tn(a, b):
    return lax.dot_general(a, b, (((0,), (0,)), ((), ())), preferred_element_type=F32)


def _row_slabs(rows):
    return [slice(i, i + ROW_SLAB) for i in range(0, rows, ROW_SLAB)]


def _silu(g):
    return g / (1.0 + jnp.exp(-g))


def _ret_inproj_kernel(x_ref, w_ref, cos_ref, sin_ref, qdec_ref, kdec_ref, q_ref, k_ref, v_ref, sg_ref):
    qk = H_A * DK_A
    half = DK_A // 2

    def rope_store(h, out_ref, rows, cos, sin, dec_ref):
        for i in range(H_A):
            lo = slice(i * DK_A, i * DK_A + half)
            hi = slice(i * DK_A + half, (i + 1) * DK_A)
            x1 = h[:, lo]
            x2 = h[:, hi]
            out_ref[rows, lo] = ((x1 * cos - x2 * sin) * dec_ref[rows, lo]).astype(BF16)
            out_ref[rows, hi] = ((x2 * cos + x1 * sin) * dec_ref[rows, hi]).astype(BF16)

    for rows in _row_slabs(x_ref.shape[0]):
        xb = x_ref[rows, :].astype(BF16)
        cos = cos_ref[rows, :]
        sin = sin_ref[rows, :]
        rope_store(_dot(xb, w_ref[:, 0:qk]), q_ref, rows, cos, sin, qdec_ref)
        rope_store(_dot(xb, w_ref[:, qk:2 * qk]), k_ref, rows, cos, sin, kdec_ref)
        v_ref[rows, :] = _dot(xb, w_ref[:, 2 * qk:2 * qk + WIDTH_A]).astype(BF16)
        sg_ref[rows, :] = _silu(_dot(xb, w_ref[:, 2 * qk + WIDTH_A:])).astype(BF16)


def _ret_inproj(x2, w, cos, sin, qdec, kdec, seq):
    t = x2.shape[0]
    tm = ROW_TILE
    nseq = seq // tm
    qk = H_A * DK_A
    row = lambda i: (i, 0)
    pos = lambda i: (i % nseq, 0)
    return pl.pallas_call(
        _ret_inproj_kernel,
        grid=(t // tm,),
        in_specs=[pl.BlockSpec((tm, D_MODEL), row),
                  _resident(w.shape),
                  pl.BlockSpec((tm, DK_A // 2), pos),
                  pl.BlockSpec((tm, DK_A // 2), pos),
                  _resident(qdec.shape), _resident(kdec.shape)],
        out_specs=[pl.BlockSpec((tm, qk), row), pl.BlockSpec((tm, qk), row),
                   pl.BlockSpec((tm, WIDTH_A), row), pl.BlockSpec((tm, WIDTH_A), row)],
        out_shape=[jax.ShapeDtypeStruct((t, qk), BF16), jax.ShapeDtypeStruct((t, qk), BF16),
                   jax.ShapeDtypeStruct((t, WIDTH_A), BF16), jax.ShapeDtypeStruct((t, WIDTH_A), BF16)],
        compiler_params=_params("arbitrary"),
        name="ret_inproj",
    )(x2, w, cos, sin, qdec, kdec)


def _retention_kernel(q_ref, k_ref, v_ref, sg_ref, causal_ref, cdec_ref, o_ref, state_ref, vs_ref):
    c = RET_CHUNK

    @pl.when(pl.program_id(1) == 0)
    def _():
        state_ref[...] = jnp.zeros_like(state_ref)
        vs_ref[...] = jnp.zeros_like(vs_ref)

    for h in range(H_A):
        qk_cols = slice(h * DK_A, (h + 1) * DK_A)
        v_cols = slice(h * DV_A, (h + 1) * DV_A)
        q = q_ref[0, :, qk_cols]
        k = k_ref[0, :, qk_cols]
        v = v_ref[0, :, v_cols]
        vs_ref[h, 0:c, :] = v
        scores = (_dot_nt(q, k) * causal_ref[...]).astype(BF16)
        o = _dot(jnp.concatenate([scores, q], axis=1), vs_ref[h])
        state = (state_ref[h] + _dot_tn(k, v)) * cdec_ref[h]
        state_ref[h] = state
        vs_ref[h, c:, :] = state.astype(BF16)

        mu = jnp.mean(o, axis=-1, keepdims=True)
        d = o - mu
        var = jnp.mean(d * d, axis=-1, keepdims=True)
        gated = d * lax.rsqrt(var + 1e-5) * sg_ref[0, :, v_cols].astype(F32)
        o_ref[0, :, v_cols] = gated.astype(BF16)


def _retention(q, k, v, sg, causal, cdec):
    b, s, _ = q.shape
    c = RET_CHUNK
    assert c == DK_A
    tok = lambda bi, ci: (bi, ci, 0)
    return pl.pallas_call(
        _retention_kernel,
        grid=(b, s // c),
        in_specs=[pl.BlockSpec((1, c, H_A * DK_A), tok), pl.BlockSpec((1, c, H_A * DK_A), tok),
                  pl.BlockSpec((1, c, WIDTH_A), tok), pl.BlockSpec((1, c, WIDTH_A), tok),
                  _resident(causal.shape), _resident(cdec.shape)],
        out_specs=pl.BlockSpec((1, c, WIDTH_A), tok),
        out_shape=jax.ShapeDtypeStruct((b, s, WIDTH_A), BF16),
        scratch_shapes=[pltpu.VMEM((H_A, DK_A, DV_A), F32), pltpu.VMEM((H_A, c + DK_A, DV_A), BF16)],
        compiler_params=_params("arbitrary", "arbitrary"),
        name="retention",
    )(q, k, v, sg, causal, cdec)


def _outproj_ln_kernel(a_ref, w_ref, x_ref, g_ref, b_ref, o_ref):
    for rows in _row_slabs(a_ref.shape[0]):
        z = DEEPNORM_ALPHA * x_ref[rows, :] + _dot(a_ref[rows, :], w_ref[...])
        mu = jnp.mean(z, axis=-1, keepdims=True)
        d = z - mu
        var = jnp.mean(d * d, axis=-1, keepdims=True)
        o_ref[rows, :] = d * lax.rsqrt(var + 1e-5) * g_ref[...] + b_ref[...]


def _outproj_ln(a, w, x2, g, bias):
    t, width = a.shape
    tm = WIDE_ROW_TILE
    row = lambda i: (i, 0)
    return pl.pallas_call(
        _outproj_ln_kernel,
        grid=(t // tm,),
        in_specs=[pl.BlockSpec((tm, width), row), _resident(w.shape),
                  pl.BlockSpec((tm, D_MODEL), row), _resident(g.shape), _resident(bias.shape)],
        out_specs=pl.BlockSpec((tm, D_MODEL), row),
        out_shape=jax.ShapeDtypeStruct((t, D_MODEL), F32),
        compiler_params=_params("arbitrary"),
        name="outproj_ln",
    )(a, w, x2, g, bias)


def _rope_pair(u, tab):
    t = u * tab
    lane = lax.broadcasted_iota(jnp.int32, t.shape, 1)
    return jnp.where(lane < QK_ROPE, t + pltpu.roll(t, QK_ROPE, 1), 0.0)


def _rms(x, g, eps=1e-6):
    return x * lax.rsqrt(jnp.mean(x * x, axis=-1, keepdims=True) + eps) * g


def _mla_inproj_kernel(x_ref, win_ref, wdn_ref, qg_ref, kvg_ref, tab_ref,
                       qn_ref, sg_ref, lat_ref, kr_ref):
    for rows in _row_slabs(x_ref.shape[0]):
        xb = x_ref[rows, :].astype(BF16)
        qn_ref[rows, :] = _rms(_dot(xb, win_ref[:, 0:Q_LORA]), qg_ref[...]).astype(BF16)
        sg_ref[rows, :] = _silu(_dot(xb, win_ref[:, Q_LORA:])).astype(BF16)
        c = _dot(xb, wdn_ref[...])
        lat_ref[rows, :] = _rms(c[:, 0:KV_LORA], kvg_ref[...]).astype(BF16)
        kr_ref[rows, :] = _rope_pair(c[:, KV_LORA:], tab_ref[rows, :]).astype(BF16)


def _mla_inproj(x2, win, wdn, qg, kvg, tab, seq):
    t = x2.shape[0]
    tm = WIDE_ROW_TILE
    nseq = seq // tm
    row = lambda i: (i, 0)
    pos = lambda i: (i % nseq, 0)
    widths = (Q_LORA, WIDTH_B, KV_LORA, 2 * QK_ROPE)
    return pl.pallas_call(
        _mla_inproj_kernel,
        grid=(t // tm,),
        in_specs=[pl.BlockSpec((tm, D_MODEL), row), _resident(win.shape), _resident(wdn.shape),
                  _resident(qg.shape), _resident(kvg.shape),
                  pl.BlockSpec((tm, 2 * QK_ROPE), pos)],
        out_specs=[pl.BlockSpec((tm, w), row) for w in widths],
        out_shape=[jax.ShapeDtypeStruct((t, w), BF16) for w in widths],
        compiler_params=_params("arbitrary"),
        name="mla_inproj",
    )(x2, win, wdn, qg, kvg, tab)


def _q_up_kernel(qn_ref, w_ref, tab_ref, q_ref):
    qn = qn_ref[...]
    tab = tab_ref[...]
    scale = (QK_NOPE + QK_ROPE) ** -0.5 * math.log2(math.e)
    for h in range(H_B):
        y = _dot(qn, w_ref[:, h * QK_PAD:(h + 1) * QK_PAD])
        q_ref[0, h, :, 0:QK_NOPE] = (y[:, 0:QK_NOPE] * scale).astype(BF16)
        q_ref[0, h, :, QK_NOPE:] = (_rope_pair(y[:, QK_NOPE:], tab) * scale).astype(BF16)


def _q_up(qn, w, tab, batch, seq):
    tm = ROW_TILE
    nseq = seq // tm
    return pl.pallas_call(
        _q_up_kernel,
        grid=(batch, nseq),
        in_specs=[pl.BlockSpec((tm, Q_LORA), lambda b, i: (b * nseq + i, 0)),
                  _resident(w.shape),
                  pl.BlockSpec((tm, 2 * QK_ROPE), lambda b, i: (i, 0))],
        out_specs=pl.BlockSpec((1, H_B, tm, QK_PAD), lambda b, i: (b, 0, i, 0)),
        out_shape=jax.ShapeDtypeStruct((batch, H_B, seq, QK_PAD), BF16),
        compiler_params=_params("arbitrary", "arbitrary"),
        name="q_up",
    )(qn, w, tab)


def _kv_up_kernel(lat_ref, kr_ref, wk_ref, wvt_ref, k_ref, vt_ref):
    lat = lat_ref[...]
    kn = _dot(lat, wk_ref[...]).astype(BF16)
    kr = kr_ref[...]
    for h in range(H_B):
        k_ref[0, h, :, 0:QK_NOPE] = kn[:, h * QK_NOPE:(h + 1) * QK_NOPE]
        k_ref[0, h, :, QK_NOPE:] = kr
    vt = _dot_nt(wvt_ref[...], lat).astype(BF16)
    vt_ref[0, :, 0] = vt.reshape(H_B, V_HEAD, vt.shape[-1])


def _kv_up(lat, kr, wk, wvt, batch, seq):
    tk = ATTN_TK
    nk = seq // tk
    row = lambda b, i: (b * nk + i, 0)
    return pl.pallas_call(
        _kv_up_kernel,
        grid=(batch, nk),
        in_specs=[pl.BlockSpec((tk, KV_LORA), row), pl.BlockSpec((tk, 2 * QK_ROPE), row),
                  _resident(wk.shape), _resident(wvt.shape)],
        out_specs=[pl.BlockSpec((1, H_B, tk, QK_PAD), lambda b, i: (b, 0, i, 0)),
                   pl.BlockSpec((1, H_B, 1, V_HEAD, tk), lambda b, i: (b, 0, i, 0, 0))],
        out_shape=[jax.ShapeDtypeStruct((batch, H_B, seq, QK_PAD), BF16),
                   jax.ShapeDtypeStruct((batch, H_B, nk, V_HEAD, tk), BF16)],
        compiler_params=_params("arbitrary", "arbitrary"),
        name="kv_up",
    )(lat, kr, wk, wvt)


def _attn_kernel(q_ref, k_ref, vt_ref, sg_ref, o_ref, sa_ref, sb_ref, m_ref, l_ref, acc_ref):
    r = pl.program_id(2)
    last = pl.num_programs(2) - 1
    tq = ATTN_TQ
    tk = ATTN_TK

    def q_rows(pair, row):
        start = pl.multiple_of((2 * pair + row) * tq, tq)
        return q_ref[0, 0, pl.ds(start, tq), :]

    def scores(q, j, nblk, s_ref, off=0):
        kb = k_ref[0, 0, pl.ds(pl.multiple_of(j * tk, tk), nblk * tk), :]
        s_ref[off:off + nblk * tk, :] = _dot_nt(kb, q)

    def update(row, j, nblk, s_ref, diag_at=None, off=0):
        s = s_ref[off:off + nblk * tk, :]
        if diag_at is not None:
            kpos = lax.broadcasted_iota(jnp.int32, s.shape, 0)
            qpos = lax.broadcasted_iota(jnp.int32, s.shape, 1)
            s = jnp.where(kpos <= qpos + diag_at, s, MASK_VALUE)
        m = m_ref[row]
        m_new = jnp.maximum(m, jnp.max(s, axis=0, keepdims=True))
        alpha = jnp.exp2(m - m_new)
        p = jnp.exp2(s - m_new)
        l_ref[row] = alpha * l_ref[row] + jnp.sum(p, axis=0, keepdims=True)
        m_ref[row] = m_new
        p = p.astype(BF16)
        pv = _dot(vt_ref[0, 0, j], p[0:tk])
        for i in range(1, nblk):
            pv += _dot(vt_ref[0, 0, j + i], p[i * tk:(i + 1) * tk])
        acc_ref[row] = alpha * acc_ref[row] + pv

    def finalize(row):
        o = (acc_ref[row] / l_ref[row]).T
        rows = slice(row * tq, (row + 1) * tq)
        o_ref[0, rows, :] = (o * sg_ref[0, rows, :].astype(F32)).astype(BF16)

    m_ref[...] = jnp.full_like(m_ref, MASK_VALUE)
    l_ref[...] = jnp.zeros_like(l_ref)
    acc_ref[...] = jnp.zeros_like(acc_ref)

    @pl.when(r == 0)
    def _():
        scores(q_rows(r, 0), 0, 1, sa_ref, off=tk)

    @pl.when(r > 0)
    def _():
        def step(t):
            j = 2 * t
            scores(q_rows(r, 1), j, 2, sb_ref)
            update(0, j, 2, sa_ref)
            scores(q_rows(r, 0), jnp.minimum(j + 2, 2 * r - 1), 2, sa_ref)
            update(1, j, 2, sb_ref)

        def body(i, carry):
            step(2 * i)
            step(2 * i + 1)
            return carry

        lax.fori_loop(0, r // 2, body, 0)

        @pl.when(r % 2 == 1)
        def _():
            step(r - 1)

    scores(q_rows(r, 1), 2 * r, 2, sb_ref)
    update(0, 2 * r, 1, sa_ref, diag_at=0, off=tk)
    scores(q_rows(jnp.minimum(r + 1, last), 0), 0, 2, sa_ref)
    update(1, 2 * r, 2, sb_ref, diag_at=tk)
    finalize(0)
    finalize(1)


def _attention(q, k, vt, sg3):
    b, h, s, _ = q.shape
    tq = ATTN_TQ
    assert tq == ATTN_TK
    nk = s // ATTN_TK
    head = lambda bi, hi, ri: (bi, hi, 0, 0)
    oblk = lambda bi, hi, ri: (bi, ri, hi)
    return pl.pallas_call(
        _attn_kernel,
        grid=(b, h, s // (2 * tq)),
        in_specs=[pl.BlockSpec((1, 1, s, QK_PAD), head),
                  pl.BlockSpec((1, 1, s, QK_PAD), head),
                  pl.BlockSpec((1, 1, nk, V_HEAD, ATTN_TK), lambda bi, hi, ri: (bi, hi, 0, 0, 0)),
                  pl.BlockSpec((1, 2 * tq, V_HEAD), oblk)],
        out_specs=pl.BlockSpec((1, 2 * tq, V_HEAD), oblk),
        out_shape=jax.ShapeDtypeStruct((b, s, WIDTH_B), BF16),
        scratch_shapes=[pltpu.VMEM((2 * ATTN_TK, tq), F32), pltpu.VMEM((2 * ATTN_TK, tq), F32),
                        pltpu.VMEM((2, 1, tq), F32), pltpu.VMEM((2, 1, tq), F32),
                        pltpu.VMEM((2, V_HEAD, tq), F32)],
        compiler_params=_params("arbitrary", "arbitrary", "arbitrary"),
        name="mla_attention",
    )(q, k, vt, sg3)


def _rope_angles(seq, half, base):
    inv = base ** (-jnp.arange(half, dtype=F32) / half)
    return jnp.arange(seq, dtype=F32)[:, None] * inv[None, :]


def _retention_decays(c, rows):
    lg = jnp.log1p(-jnp.exp2(-5.0 - jnp.arange(H_A, dtype=F32)))
    idx = jnp.arange(c, dtype=F32)
    up = jnp.exp((idx + 1.0)[:, None] * lg[None, :])
    down = jnp.exp(-(idx + 1.0)[:, None] * lg[None, :]) * DK_A ** -0.5
    widen = lambda t: jnp.tile(jnp.repeat(t, DK_A, axis=1), (rows // c, 1))
    causal = (idx[:, None] >= idx[None, :]).astype(F32)
    cdec = jnp.broadcast_to(jnp.exp(c * lg)[:, None, None], (H_A, 1, DV_A))
    return widen(up), widen(down), causal, cdec


def _swap_halves(w):
    half = w.shape[-1] // 2
    return jnp.concatenate([w[..., half:], w[..., :half]], axis=-1)


def kernel(x, a_w_in, a_w_out, b_w_in, b_q_norm, b_w_uq, b_w_out, kv_w_down, kv_norm, kv_w_up,
           ln_g, ln_b):
    batch, seq, _ = x.shape
    t = batch * seq
    x2 = x.reshape(t, D_MODEL)

    ang_a = _rope_angles(seq, DK_A // 2, ROPE_BASE_A)
    cos_a, sin_a = jnp.cos(ang_a), jnp.sin(ang_a)
    ang_b = _rope_angles(seq, QK_ROPE // 2, ROPE_BASE_B)
    cos_b, sin_b = jnp.cos(ang_b), jnp.sin(ang_b)
    tab_b = jnp.concatenate([cos_b, cos_b, -sin_b, sin_b], axis=-1)

    qdec, kdec, causal, cdec = _retention_decays(RET_CHUNK, ROW_TILE)
    q, k, v, sg = _ret_inproj(x2, a_w_in[0].astype(BF16), cos_a, sin_a, qdec, kdec, seq)
    shape3 = lambda a: a.reshape(batch, seq, a.shape[-1])
    o = _retention(shape3(q), shape3(k), shape3(v), shape3(sg), causal, cdec)
    x1 = _outproj_ln(o.reshape(t, WIDTH_A), a_w_out[0].astype(BF16), x2,
                     ln_g[0][None, :], ln_b[0][None, :])

    w_rope = kv_w_down[:, KV_LORA:]
    wdn = jnp.concatenate([kv_w_down[:, :KV_LORA], w_rope, _swap_halves(w_rope)], axis=-1)
    qn, sgb, lat, kr = _mla_inproj(x1, b_w_in[0].astype(BF16), wdn.astype(BF16),
                                   b_q_norm[0][None, :], kv_norm[None, :], tab_b, seq)

    wuq = b_w_uq[0].reshape(Q_LORA, H_B, QK_NOPE + QK_ROPE)
    wuq = jnp.concatenate([wuq, _swap_halves(wuq[..., QK_NOPE:])], axis=-1)
    qh = _q_up(qn, wuq.reshape(Q_LORA, H_B * QK_PAD).astype(BF16), tab_b, batch, seq)

    wup = kv_w_up.reshape(KV_LORA, H_B, QK_NOPE + V_HEAD)
    wk = wup[..., :QK_NOPE].reshape(KV_LORA, H_B * QK_NOPE).astype(BF16)
    wvt = wup[..., QK_NOPE:].reshape(KV_LORA, H_B * V_HEAD).T.astype(BF16)
    kh, vt = _kv_up(lat, kr, wk, wvt, batch, seq)

    ob = _attention(qh, kh, vt, sgb.reshape(batch, seq, WIDTH_B))
    out = _outproj_ln(ob.reshape(t, WIDTH_B), b_w_out[0].astype(BF16), x1,
                      ln_g[1][None, :], ln_b[1][None, :])
    return out.reshape(batch, seq, D_MODEL)
```

```python
import functools
import math

import jax
import jax.numpy as jnp
from jax import lax
from jax.experimental import pallas as pl
from jax.experimental.pallas import tpu as pltpu

D_MODEL = 1024
DEPTH = 2

H_A = 4
DK_A = D_MODEL // H_A
DV_A = 2 * DK_A
WIDTH_A = H_A * DV_A
ROPE_BASE_A = 10000.0

H_B = 16
QK_NOPE = 128
QK_ROPE = 64
V_HEAD = 128
Q_LORA = 768
KV_LORA = 512
WIDTH_B = H_B * V_HEAD
ROPE_BASE_B = 10000.0
QK_PAD = 256
V_ROWS = V_HEAD + 16

DEEPNORM_ALPHA = (2.0 * DEPTH) ** 0.25

RET_CHUNK = 256
ROW_TILE = 512
WIDE_ROW_TILE = 1024
ROW_SLAB = 256
ATTN_TQ = 512
ATTN_TK = 512
MASK_VALUE = -1e30

VMEM_LIMIT_BYTES = 56 * 1024 * 1024

BF16 = jnp.bfloat16
F32 = jnp.float32


def _params(*semantics):
    return pltpu.CompilerParams(dimension_semantics=semantics,
                                vmem_limit_bytes=VMEM_LIMIT_BYTES)


def _resident(shape):
    zeros = (0,) * len(shape)
    return pl.BlockSpec(shape, lambda *_: zeros, pipeline_mode=pl.Buffered(1))


def _dot(a, b):
    return jnp.dot(a, b, preferred_element_type=F32)


def _dot_nt(a, b):
    return lax.dot_general(a, b, (((1,), (1,)), ((), ())), preferred_element_type=F32)


def _dot_tn(a, b):
    return lax.dot_general(a, b, (((0,), (0,)), ((), ())), preferred_element_type=F32)


def _row_slabs(rows):
    return [slice(i, i + ROW_SLAB) for i in range(0, rows, ROW_SLAB)]


def _silu(g):
    return g / (1.0 + jnp.exp(-g))


def _ret_inproj_kernel(x_ref, w_ref, cos_ref, sin_ref, qdec_ref, kdec_ref, q_ref, k_ref, v_ref, sg_ref):
    qk = H_A * DK_A
    half = DK_A // 2

    def rope_store(h, out_ref, rows, cos, sin, dec_ref):
        for i in range(H_A):
            lo = slice(i * DK_A, i * DK_A + half)
            hi = slice(i * DK_A + half, (i + 1) * DK_A)
            x1 = h[:, lo]
            x2 = h[:, hi]
            out_ref[rows, lo] = ((x1 * cos - x2 * sin) * dec_ref[rows, lo]).astype(BF16)
            out_ref[rows, hi] = ((x2 * cos + x1 * sin) * dec_ref[rows, hi]).astype(BF16)

    for rows in _row_slabs(x_ref.shape[0]):
        xb = x_ref[rows, :].astype(BF16)
        cos = cos_ref[rows, :]
        sin = sin_ref[rows, :]
        rope_store(_dot(xb, w_ref[:, 0:qk]), q_ref, rows, cos, sin, qdec_ref)
        rope_store(_dot(xb, w_ref[:, qk:2 * qk]), k_ref, rows, cos, sin, kdec_ref)
        v_ref[rows, :] = _dot(xb, w_ref[:, 2 * qk:2 * qk + WIDTH_A]).astype(BF16)
        sg_ref[rows, :] = _silu(_dot(xb, w_ref[:, 2 * qk + WIDTH_A:])).astype(BF16)


def _ret_inproj(x2, w, cos, sin, qdec, kdec, seq):
    t = x2.shape[0]
    tm = ROW_TILE
    nseq = seq // tm
    qk = H_A * DK_A
    row = lambda i: (i, 0)
    pos = lambda i: (i % nseq, 0)
    return pl.pallas_call(
        _ret_inproj_kernel,
        grid=(t // tm,),
        in_specs=[pl.BlockSpec((tm, D_MODEL), row),
                  _resident(w.shape),
                  pl.BlockSpec((tm, DK_A // 2), pos),
                  pl.BlockSpec((tm, DK_A // 2), pos),
                  _resident(qdec.shape), _resident(kdec.shape)],
        out_specs=[pl.BlockSpec((tm, qk), row), pl.BlockSpec((tm, qk), row),
                   pl.BlockSpec((tm, WIDTH_A), row), pl.BlockSpec((tm, WIDTH_A), row)],
        out_shape=[jax.ShapeDtypeStruct((t, qk), BF16), jax.ShapeDtypeStruct((t, qk), BF16),
                   jax.ShapeDtypeStruct((t, WIDTH_A), BF16), jax.ShapeDtypeStruct((t, WIDTH_A), BF16)],
        compiler_params=_params("arbitrary"),
        name="ret_inproj",
    )(x2, w, cos, sin, qdec, kdec)


def _retention_kernel(q_ref, k_ref, v_ref, sg_ref, causal_ref, cdec_ref, o_ref, state_ref, vs_ref):
    c = RET_CHUNK

    @pl.when(pl.program_id(1) == 0)
    def _():
        state_ref[...] = jnp.zeros_like(state_ref)
        vs_ref[...] = jnp.zeros_like(vs_ref)

    for h in range(H_A):
        qk_cols = slice(h * DK_A, (h + 1) * DK_A)
        v_cols = slice(h * DV_A, (h + 1) * DV_A)
        q = q_ref[0, :, qk_cols]
        k = k_ref[0, :, qk_cols]
        v = v_ref[0, :, v_cols]
        vs_ref[h, 0:c, :] = v
        scores = (_dot_nt(q, k) * causal_ref[...]).astype(BF16)
        o = _dot(jnp.concatenate([scores, q], axis=1), vs_ref[h])
        state = (state_ref[h] + _dot_tn(k, v)) * cdec_ref[h]
        state_ref[h] = state
        vs_ref[h, c:, :] = state.astype(BF16)

        mu = jnp.mean(o, axis=-1, keepdims=True)
        d = o - mu
        var = jnp.mean(d * d, axis=-1, keepdims=True)
        gated = d * lax.rsqrt(var + 1e-5) * sg_ref[0, :, v_cols].astype(F32)
        o_ref[0, :, v_cols] = gated.astype(BF16)


def _retention(q, k, v, sg, causal, cdec):
    b, s, _ = q.shape
    c = RET_CHUNK
    assert c == DK_A
    tok = lambda bi, ci: (bi, ci, 0)
    return pl.pallas_call(
        _retention_kernel,
        grid=(b, s // c),
        in_specs=[pl.BlockSpec((1, c, H_A * DK_A), tok), pl.BlockSpec((1, c, H_A * DK_A), tok),
                  pl.BlockSpec((1, c, WIDTH_A), tok), pl.BlockSpec((1, c, WIDTH_A), tok),
                  _resident(causal.shape), _resident(cdec.shape)],
        out_specs=pl.BlockSpec((1, c, WIDTH_A), tok),
        out_shape=jax.ShapeDtypeStruct((b, s, WIDTH_A), BF16),
        scratch_shapes=[pltpu.VMEM((H_A, DK_A, DV_A), F32), pltpu.VMEM((H_A, c + DK_A, DV_A), BF16)],
        compiler_params=_params("arbitrary", "arbitrary"),
        name="retention",
    )(q, k, v, sg, causal, cdec)


def _outproj_ln_kernel(a_ref, w_ref, x_ref, g_ref, b_ref, o_ref):
    for rows in _row_slabs(a_ref.shape[0]):
        z = DEEPNORM_ALPHA * x_ref[rows, :] + _dot(a_ref[rows, :], w_ref[...])
        mu = jnp.mean(z, axis=-1, keepdims=True)
        d = z - mu
        var = jnp.mean(d * d, axis=-1, keepdims=True)
        o_ref[rows, :] = d * lax.rsqrt(var + 1e-5) * g_ref[...] + b_ref[...]


def _outproj_ln(a, w, x2, g, bias):
    t, width = a.shape
    tm = WIDE_ROW_TILE
    row = lambda i: (i, 0)
    return pl.pallas_call(
        _outproj_ln_kernel,
        grid=(t // tm,),
        in_specs=[pl.BlockSpec((tm, width), row), _resident(w.shape),
                  pl.BlockSpec((tm, D_MODEL), row), _resident(g.shape), _resident(bias.shape)],
        out_specs=pl.BlockSpec((tm, D_MODEL), row),
        out_shape=jax.ShapeDtypeStruct((t, D_MODEL), F32),
        compiler_params=_params("arbitrary"),
        name="outproj_ln",
    )(a, w, x2, g, bias)


def _rope_pair(u, tab):
    t = u * tab
    lane = lax.broadcasted_iota(jnp.int32, t.shape, 1)
    return jnp.where(lane < QK_ROPE, t + pltpu.roll(t, QK_ROPE, 1), 0.0)


def _rms(x, g, eps=1e-6):
    return x * lax.rsqrt(jnp.mean(x * x, axis=-1, keepdims=True) + eps) * g


def _mla_inproj_kernel(x_ref, win_ref, wdn_ref, qg_ref, kvg_ref, tab_ref,
                       qn_ref, sg_ref, lat_ref, kr_ref):
    for rows in _row_slabs(x_ref.shape[0]):
        xb = x_ref[rows, :].astype(BF16)
        qn_ref[rows, :] = _rms(_dot(xb, win_ref[:, 0:Q_LORA]), qg_ref[...]).astype(BF16)
        sg_ref[rows, :] = _silu(_dot(xb, win_ref[:, Q_LORA:])).astype(BF16)
        c = _dot(xb, wdn_ref[...])
        lat_ref[rows, :] = _rms(c[:, 0:KV_LORA], kvg_ref[...]).astype(BF16)
        kr_ref[rows, :] = _rope_pair(c[:, KV_LORA:], tab_ref[rows, :]).astype(BF16)


def _mla_inproj(x2, win, wdn, qg, kvg, tab, seq):
    t = x2.shape[0]
    tm = WIDE_ROW_TILE
    nseq = seq // tm
    row = lambda i: (i, 0)
    pos = lambda i: (i % nseq, 0)
    widths = (Q_LORA, WIDTH_B, KV_LORA, 2 * QK_ROPE)
    return pl.pallas_call(
        _mla_inproj_kernel,
        grid=(t // tm,),
        in_specs=[pl.BlockSpec((tm, D_MODEL), row), _resident(win.shape), _resident(wdn.shape),
                  _resident(qg.shape), _resident(kvg.shape),
                  pl.BlockSpec((tm, 2 * QK_ROPE), pos)],
        out_specs=[pl.BlockSpec((tm, w), row) for w in widths],
        out_shape=[jax.ShapeDtypeStruct((t, w), BF16) for w in widths],
        compiler_params=_params("arbitrary"),
        name="mla_inproj",
    )(x2, win, wdn, qg, kvg, tab)


def _q_up_kernel(qn_ref, w_ref, tab_ref, q_ref):
    qn = qn_ref[...]
    tab = tab_ref[...]
    scale = (QK_NOPE + QK_ROPE) ** -0.5 * math.log2(math.e)
    for h in range(H_B):
        y = _dot(qn, w_ref[:, h * QK_PAD:(h + 1) * QK_PAD])
        q_ref[0, h, :, 0:QK_NOPE] = (y[:, 0:QK_NOPE] * scale).astype(BF16)
        q_ref[0, h, :, QK_NOPE:] = (_rope_pair(y[:, QK_NOPE:], tab) * scale).astype(BF16)


def _q_up(qn, w, tab, batch, seq):
    tm = ROW_TILE
    nseq = seq // tm
    return pl.pallas_call(
        _q_up_kernel,
        grid=(batch, nseq),
        in_specs=[pl.BlockSpec((tm, Q_LORA), lambda b, i: (b * nseq + i, 0)),
                  _resident(w.shape),
                  pl.BlockSpec((tm, 2 * QK_ROPE), lambda b, i: (i, 0))],
        out_specs=pl.BlockSpec((1, H_B, tm, QK_PAD), lambda b, i: (b, 0, i, 0)),
        out_shape=jax.ShapeDtypeStruct((batch, H_B, seq, QK_PAD), BF16),
        compiler_params=_params("arbitrary", "arbitrary"),
        name="q_up",
    )(qn, w, tab)


def _kv_up_kernel(lat_ref, kr_ref, wk_ref, wvt_ref, k_ref, vt_ref):
    lat = lat_ref[...]
    kn = _dot(lat, wk_ref[...]).astype(BF16)
    kr = kr_ref[...]
    for h in range(H_B):
        k_ref[0, h, :, 0:QK_NOPE] = kn[:, h * QK_NOPE:(h + 1) * QK_NOPE]
        k_ref[0, h, :, QK_NOPE:] = kr
    vt = _dot_nt(wvt_ref[...], lat).astype(BF16)
    tk = vt.shape[-1]
    vt_ref[0, :, 0, 0:V_HEAD, :] = vt.reshape(H_B, V_HEAD, tk)
    extra = lax.broadcasted_iota(jnp.int32, (H_B, V_ROWS - V_HEAD, tk), 1) == 0
    vt_ref[0, :, 0, V_HEAD:, :] = extra.astype(BF16)


def _kv_up(lat, kr, wk, wvt, batch, seq):
    tk = ATTN_TK
    nk = seq // tk
    row = lambda b, i: (b * nk + i, 0)
    return pl.pallas_call(
        _kv_up_kernel,
        grid=(batch, nk),
        in_specs=[pl.BlockSpec((tk, KV_LORA), row), pl.BlockSpec((tk, 2 * QK_ROPE), row),
                  _resident(wk.shape), _resident(wvt.shape)],
        out_specs=[pl.BlockSpec((1, H_B, tk, QK_PAD), lambda b, i: (b, 0, i, 0)),
                   pl.BlockSpec((1, H_B, 1, V_ROWS, tk), lambda b, i: (b, 0, i, 0, 0))],
        out_shape=[jax.ShapeDtypeStruct((batch, H_B, seq, QK_PAD), BF16),
                   jax.ShapeDtypeStruct((batch, H_B, nk, V_ROWS, tk), BF16)],
        compiler_params=_params("arbitrary", "arbitrary"),
        name="kv_up",
    )(lat, kr, wk, wvt)


def _attn_kernel(q_ref, k_ref, vt_ref, sg_ref, o_ref, sa_ref, sb_ref, mxa_ref, mxb_ref, m_ref, acc_ref):
    r = pl.program_id(2)
    last = pl.num_programs(2) - 1
    tq = ATTN_TQ
    tk = ATTN_TK

    def q_rows(pair, row):
        start = pl.multiple_of((2 * pair + row) * tq, tq)
        return q_ref[0, 0, pl.ds(start, tq), :]

    def scores(q, j, nblk, s_ref, mx_ref, off=0):
        kb = k_ref[0, 0, pl.ds(pl.multiple_of(j * tk, tk), nblk * tk), :]
        s = _dot_nt(kb, q)
        s_ref[off:off + nblk * tk, :] = s
        mx_ref[...] = jnp.max(s, axis=0, keepdims=True)

    def update(row, j, nblk, s_ref, mx_ref, diag_at=None, off=0):
        s = s_ref[off:off + nblk * tk, :]
        if diag_at is None:
            block_max = mx_ref[...]
        else:
            kpos = lax.broadcasted_iota(jnp.int32, s.shape, 0)
            qpos = lax.broadcasted_iota(jnp.int32, s.shape, 1)
            s = jnp.where(kpos <= qpos + diag_at, s, MASK_VALUE)
            block_max = jnp.max(s, axis=0, keepdims=True)
        m = m_ref[row]
        m_new = jnp.maximum(m, block_max)
        alpha = jnp.exp2(m - m_new)
        p = jnp.exp2(s - m_new).astype(BF16)
        m_ref[row] = m_new
        pv = _dot(vt_ref[0, 0, j], p[0:tk])
        for i in range(1, nblk):
            pv += _dot(vt_ref[0, 0, j + i], p[i * tk:(i + 1) * tk])
        acc_ref[row] = alpha * acc_ref[row] + pv

    def finalize(row):
        o = (acc_ref[row, 0:V_HEAD, :] / acc_ref[row, V_HEAD:V_HEAD + 1, :]).T
        rows = slice(row * tq, (row + 1) * tq)
        o_ref[0, rows, :] = (o * sg_ref[0, rows, :].astype(F32)).astype(BF16)

    m_ref[...] = jnp.full_like(m_ref, MASK_VALUE)
    acc_ref[...] = jnp.zeros_like(acc_ref)

    @pl.when(r == 0)
    def _():
        scores(q_rows(r, 0), 0, 1, sa_ref, mxa_ref, off=tk)

    @pl.when(r > 0)
    def _():
        def step(t):
            j = 2 * t
            scores(q_rows(r, 1), j, 2, sb_ref, mxb_ref)
            update(0, j, 2, sa_ref, mxa_ref)
            scores(q_rows(r, 0), jnp.minimum(j + 2, 2 * r - 1), 2, sa_ref, mxa_ref)
            update(1, j, 2, sb_ref, mxb_ref)

        def body(i, carry):
            step(2 * i)
            step(2 * i + 1)
            return carry

        lax.fori_loop(0, r // 2, body, 0)

        @pl.when(r % 2 == 1)
        def _():
            step(r - 1)

    scores(q_rows(r, 1), 2 * r, 2, sb_ref, mxb_ref)
    update(0, 2 * r, 1, sa_ref, mxa_ref, diag_at=0, off=tk)
    scores(q_rows(jnp.minimum(r + 1, last), 0), 0, 2, sa_ref, mxa_ref)
    update(1, 2 * r, 2, sb_ref, mxb_ref, diag_at=tk)
    finalize(0)
    finalize(1)


def _attention(q, k, vt, sg3):
    b, h, s, _ = q.shape
    tq = ATTN_TQ
    assert tq == ATTN_TK
    nk = s // ATTN_TK
    head = lambda bi, hi, ri: (bi, hi, 0, 0)
    oblk = lambda bi, hi, ri: (bi, ri, hi)
    return pl.pallas_call(
        _attn_kernel,
        grid=(b, h, s // (2 * tq)),
        in_specs=[pl.BlockSpec((1, 1, s, QK_PAD), head),
                  pl.BlockSpec((1, 1, s, QK_PAD), head),
                  pl.BlockSpec((1, 1, nk, V_ROWS, ATTN_TK), lambda bi, hi, ri: (bi, hi, 0, 0, 0)),
                  pl.BlockSpec((1, 2 * tq, V_HEAD), oblk)],
        out_specs=pl.BlockSpec((1, 2 * tq, V_HEAD), oblk),
        out_shape=jax.ShapeDtypeStruct((b, s, WIDTH_B), BF16),
        scratch_shapes=[pltpu.VMEM((2 * ATTN_TK, tq), F32), pltpu.VMEM((2 * ATTN_TK, tq), F32),
                        pltpu.VMEM((1, tq), F32), pltpu.VMEM((1, tq), F32),
                        pltpu.VMEM((2, 1, tq), F32), pltpu.VMEM((2, V_ROWS, tq), F32)],
        compiler_params=_params("arbitrary", "arbitrary", "arbitrary"),
        name="mla_attention",
    )(q, k, vt, sg3)


def _rope_angles(seq, half, base):
    inv = base ** (-jnp.arange(half, dtype=F32) / half)
    return jnp.arange(seq, dtype=F32)[:, None] * inv[None, :]


def _retention_decays(c, rows):
    lg = jnp.log1p(-jnp.exp2(-5.0 - jnp.arange(H_A, dtype=F32)))
    idx = jnp.arange(c, dtype=F32)
    up = jnp.exp((idx + 1.0)[:, None] * lg[None, :])
    down = jnp.exp(-(idx + 1.0)[:, None] * lg[None, :]) * DK_A ** -0.5
    widen = lambda t: jnp.tile(jnp.repeat(t, DK_A, axis=1), (rows // c, 1))
    causal = (idx[:, None] >= idx[None, :]).astype(F32)
    cdec = jnp.broadcast_to(jnp.exp(c * lg)[:, None, None], (H_A, 1, DV_A))
    return widen(up), widen(down), causal, cdec


def _swap_halves(w):
    half = w.shape[-1] // 2
    return jnp.concatenate([w[..., half:], w[..., :half]], axis=-1)


def kernel(x, a_w_in, a_w_out, b_w_in, b_q_norm, b_w_uq, b_w_out, kv_w_down, kv_norm, kv_w_up,
           ln_g, ln_b):
    batch, seq, _ = x.shape
    t = batch * seq
    x2 = x.reshape(t, D_MODEL)

    ang_a = _rope_angles(seq, DK_A // 2, ROPE_BASE_A)
    cos_a, sin_a = jnp.cos(ang_a), jnp.sin(ang_a)
    ang_b = _rope_angles(seq, QK_ROPE // 2, ROPE_BASE_B)
    cos_b, sin_b = jnp.cos(ang_b), jnp.sin(ang_b)
    tab_b = jnp.concatenate([cos_b, cos_b, -sin_b, sin_b], axis=-1)

    qdec, kdec, causal, cdec = _retention_decays(RET_CHUNK, ROW_TILE)
    q, k, v, sg = _ret_inproj(x2, a_w_in[0].astype(BF16), cos_a, sin_a, qdec, kdec, seq)
    shape3 = lambda a: a.reshape(batch, seq, a.shape[-1])
    o = _retention(shape3(q), shape3(k), shape3(v), shape3(sg), causal, cdec)
    x1 = _outproj_ln(o.reshape(t, WIDTH_A), a_w_out[0].astype(BF16), x2,
                     ln_g[0][None, :], ln_b[0][None, :])

    w_rope = kv_w_down[:, KV_LORA:]
    wdn = jnp.concatenate([kv_w_down[:, :KV_LORA], w_rope, _swap_halves(w_rope)], axis=-1)
    qn, sgb, lat, kr = _mla_inproj(x1, b_w_in[0].astype(BF16), wdn.astype(BF16),
                                   b_q_norm[0][None, :], kv_norm[None, :], tab_b, seq)

    wuq = b_w_uq[0].reshape(Q_LORA, H_B, QK_NOPE + QK_ROPE)
    wuq = jnp.concatenate([wuq, _swap_halves(wuq[..., QK_NOPE:])], axis=-1)
    qh = _q_up(qn, wuq.reshape(Q_LORA, H_B * QK_PAD).astype(BF16), tab_b, batch, seq)

    wup = kv_w_up.reshape(KV_LORA, H_B, QK_NOPE + V_HEAD)
    wk = wup[..., :QK_NOPE].reshape(KV_LORA, H_B * QK_NOPE).astype(BF16)
    wvt = wup[..., QK_NOPE:].reshape(KV_LORA, H_B * V_HEAD).T.astype(BF16)
    kh, vt = _kv_up(lat, kr, wk, wvt, batch, seq)

    ob = _attention(qh, kh, vt, sgb.reshape(batch, seq, WIDTH_B))
    out = _outproj_ln(ob.reshape(t, WIDTH_B), b_w_out[0].astype(BF16), x1,
                      ln_g[1][None, :], ln_b[1][None, :])
    return out.reshape(batch, seq, D_MODEL)
```

```python
import functools
import math

import jax
import jax.numpy as jnp
from jax import lax
from jax.experimental import pallas as pl
from jax.experimental.pallas import tpu as pltpu

D_MODEL = 1024
DEPTH = 2

H_A = 4
DK_A = D_MODEL // H_A
DV_A = 2 * DK_A
WIDTH_A = H_A * DV_A
ROPE_BASE_A = 10000.0

H_B = 16
QK_NOPE = 128
QK_ROPE = 64
V_HEAD = 128
Q_LORA = 768
KV_LORA = 512
WIDTH_B = H_B * V_HEAD
ROPE_BASE_B = 10000.0
QK_PAD = 256
V_ROWS = V_HEAD + 16

DEEPNORM_ALPHA = (2.0 * DEPTH) ** 0.25

RET_CHUNK = 256
ROW_TILE = 512
WIDE_ROW_TILE = 1024
ROW_SLAB = 256
ATTN_TQ = 512
ATTN_TK = 512
MASK_VALUE = -1e30

VMEM_LIMIT_BYTES = 56 * 1024 * 1024

BF16 = jnp.bfloat16
F32 = jnp.float32


def _params(*semantics):
    return pltpu.CompilerParams(dimension_semantics=semantics,
                                vmem_limit_bytes=VMEM_LIMIT_BYTES)


def _resident(shape):
    zeros = (0,) * len(shape)
    return pl.BlockSpec(shape, lambda *_: zeros, pipeline_mode=pl.Buffered(1))


def _dot(a, b):
    return jnp.dot(a, b, preferred_element_type=F32)


def _dot_nt(a, b):
    return lax.dot_general(a, b, (((1,), (1,)), ((), ())), preferred_element_type=F32)


def _dot_tn(a, b):
    return lax.dot_general(a, b, (((0,), (0,)), ((), ())), preferred_element_type=F32)


def _row_slabs(rows):
    return [slice(i, i + ROW_SLAB) for i in range(0, rows, ROW_SLAB)]


def _silu(g):
    return g / (1.0 + jnp.exp(-g))


def _ret_inproj_kernel(x_ref, w_ref, cos_ref, sin_ref, qdec_ref, kdec_ref, q_ref, k_ref, v_ref, sg_ref):
    qk = H_A * DK_A
    half = DK_A // 2

    def rope_store(h, out_ref, rows, cos, sin, dec_ref):
        for i in range(H_A):
            lo = slice(i * DK_A, i * DK_A + half)
            hi = slice(i * DK_A + half, (i + 1) * DK_A)
            x1 = h[:, lo]
            x2 = h[:, hi]
            out_ref[rows, lo] = ((x1 * cos - x2 * sin) * dec_ref[rows, lo]).astype(BF16)
            out_ref[rows, hi] = ((x2 * cos + x1 * sin) * dec_ref[rows, hi]).astype(BF16)

    for rows in _row_slabs(x_ref.shape[0]):
        xb = x_ref[rows, :].astype(BF16)
        cos = cos_ref[rows, :]
        sin = sin_ref[rows, :]
        rope_store(_dot(xb, w_ref[:, 0:qk]), q_ref, rows, cos, sin, qdec_ref)
        rope_store(_dot(xb, w_ref[:, qk:2 * qk]), k_ref, rows, cos, sin, kdec_ref)
        v_ref[rows, :] = _dot(xb, w_ref[:, 2 * qk:2 * qk + WIDTH_A]).astype(BF16)
        sg_ref[rows, :] = _silu(_dot(xb, w_ref[:, 2 * qk + WIDTH_A:])).astype(BF16)


def _ret_inproj(x2, w, cos, sin, qdec, kdec, seq):
    t = x2.shape[0]
    tm = ROW_TILE
    nseq = seq // tm
    qk = H_A * DK_A
    row = lambda i: (i, 0)
    pos = lambda i: (i % nseq, 0)
    return pl.pallas_call(
        _ret_inproj_kernel,
        grid=(t // tm,),
        in_specs=[pl.BlockSpec((tm, D_MODEL), row),
                  _resident(w.shape),
                  pl.BlockSpec((tm, DK_A // 2), pos),
                  pl.BlockSpec((tm, DK_A // 2), pos),
                  _resident(qdec.shape), _resident(kdec.shape)],
        out_specs=[pl.BlockSpec((tm, qk), row), pl.BlockSpec((tm, qk), row),
                   pl.BlockSpec((tm, WIDTH_A), row), pl.BlockSpec((tm, WIDTH_A), row)],
        out_shape=[jax.ShapeDtypeStruct((t, qk), BF16), jax.ShapeDtypeStruct((t, qk), BF16),
                   jax.ShapeDtypeStruct((t, WIDTH_A), BF16), jax.ShapeDtypeStruct((t, WIDTH_A), BF16)],
        compiler_params=_params("arbitrary"),
        name="ret_inproj",
    )(x2, w, cos, sin, qdec, kdec)


def _retention_kernel(q_ref, k_ref, v_ref, sg_ref, causal_ref, cdec_ref, o_ref, state_ref, sbf_ref):
    @pl.when(pl.program_id(1) == 0)
    def _():
        state_ref[...] = jnp.zeros_like(state_ref)
        sbf_ref[...] = jnp.zeros_like(sbf_ref)

    for h in range(H_A):
        qk_cols = slice(h * DK_A, (h + 1) * DK_A)
        v_cols = slice(h * DV_A, (h + 1) * DV_A)
        q = q_ref[0, :, qk_cols]
        k = k_ref[0, :, qk_cols]
        v = v_ref[0, :, v_cols]
        scores = (_dot_nt(q, k) * causal_ref[...]).astype(BF16)
        o = _dot(scores, v) + _dot(q, sbf_ref[h])
        state = (state_ref[h] + _dot_tn(k, v)) * cdec_ref[h]
        state_ref[h] = state
        sbf_ref[h] = state.astype(BF16)

        mu = jnp.mean(o, axis=-1, keepdims=True)
        d = o - mu
        var = jnp.mean(d * d, axis=-1, keepdims=True)
        gated = d * lax.rsqrt(var + 1e-5) * sg_ref[0, :, v_cols].astype(F32)
        o_ref[0, :, v_cols] = gated.astype(BF16)


def _retention(q, k, v, sg, causal, cdec):
    b, s, _ = q.shape
    c = RET_CHUNK
    tok = lambda bi, ci: (bi, ci, 0)
    return pl.pallas_call(
        _retention_kernel,
        grid=(b, s // c),
        in_specs=[pl.BlockSpec((1, c, H_A * DK_A), tok), pl.BlockSpec((1, c, H_A * DK_A), tok),
                  pl.BlockSpec((1, c, WIDTH_A), tok), pl.BlockSpec((1, c, WIDTH_A), tok),
                  _resident(causal.shape), _resident(cdec.shape)],
        out_specs=pl.BlockSpec((1, c, WIDTH_A), tok),
        out_shape=jax.ShapeDtypeStruct((b, s, WIDTH_A), BF16),
        scratch_shapes=[pltpu.VMEM((H_A, DK_A, DV_A), F32), pltpu.VMEM((H_A, DK_A, DV_A), BF16)],
        compiler_params=_params("arbitrary", "arbitrary"),
        name="retention",
    )(q, k, v, sg, causal, cdec)


def _outproj_ln_kernel(a_ref, w_ref, x_ref, g_ref, b_ref, o_ref):
    for rows in _row_slabs(a_ref.shape[0]):
        z = DEEPNORM_ALPHA * x_ref[rows, :] + _dot(a_ref[rows, :], w_ref[...])
        mu = jnp.mean(z, axis=-1, keepdims=True)
        d = z - mu
        var = jnp.mean(d * d, axis=-1, keepdims=True)
        o_ref[rows, :] = d * lax.rsqrt(var + 1e-5) * g_ref[...] + b_ref[...]


def _outproj_ln(a, w, x2, g, bias):
    t, width = a.shape
    tm = WIDE_ROW_TILE
    row = lambda i: (i, 0)
    return pl.pallas_call(
        _outproj_ln_kernel,
        grid=(t // tm,),
        in_specs=[pl.BlockSpec((tm, width), row), _resident(w.shape),
                  pl.BlockSpec((tm, D_MODEL), row), _resident(g.shape), _resident(bias.shape)],
        out_specs=pl.BlockSpec((tm, D_MODEL), row),
        out_shape=jax.ShapeDtypeStruct((t, D_MODEL), F32),
        compiler_params=_params("arbitrary"),
        name="outproj_ln",
    )(a, w, x2, g, bias)


def _rope_pair(u, tab):
    t = u * tab
    lane = lax.broadcasted_iota(jnp.int32, t.shape, 1)
    return jnp.where(lane < QK_ROPE, t + pltpu.roll(t, QK_ROPE, 1), 0.0)


def _rms(x, g, eps=1e-6):
    return x * lax.rsqrt(jnp.mean(x * x, axis=-1, keepdims=True) + eps) * g


def _mla_inproj_kernel(x_ref, win_ref, wdn_ref, qg_ref, kvg_ref, tab_ref,
                       qn_ref, sg_ref, lat_ref, kr_ref):
    for rows in _row_slabs(x_ref.shape[0]):
        xb = x_ref[rows, :].astype(BF16)
        qn_ref[rows, :] = _rms(_dot(xb, win_ref[:, 0:Q_LORA]), qg_ref[...]).astype(BF16)
        sg_ref[rows, :] = _silu(_dot(xb, win_ref[:, Q_LORA:])).astype(BF16)
        c = _dot(xb, wdn_ref[...])
        lat_ref[rows, :] = _rms(c[:, 0:KV_LORA], kvg_ref[...]).astype(BF16)
        kr_ref[rows, :] = _rope_pair(c[:, KV_LORA:], tab_ref[rows, :]).astype(BF16)


def _mla_inproj(x2, win, wdn, qg, kvg, tab, seq):
    t = x2.shape[0]
    tm = WIDE_ROW_TILE
    nseq = seq // tm
    row = lambda i: (i, 0)
    pos = lambda i: (i % nseq, 0)
    widths = (Q_LORA, WIDTH_B, KV_LORA, 2 * QK_ROPE)
    return pl.pallas_call(
        _mla_inproj_kernel,
        grid=(t // tm,),
        in_specs=[pl.BlockSpec((tm, D_MODEL), row), _resident(win.shape), _resident(wdn.shape),
                  _resident(qg.shape), _resident(kvg.shape),
                  pl.BlockSpec((tm, 2 * QK_ROPE), pos)],
        out_specs=[pl.BlockSpec((tm, w), row) for w in widths],
        out_shape=[jax.ShapeDtypeStruct((t, w), BF16) for w in widths],
        compiler_params=_params("arbitrary"),
        name="mla_inproj",
    )(x2, win, wdn, qg, kvg, tab)


def _rope_lane_pairs(c, tab):
    lane = lax.broadcasted_iota(jnp.int32, c.shape, 1)
    half = QK_ROPE // 2
    swapped = jnp.where(lane % QK_ROPE < half, pltpu.roll(c, 128 - half, 1), pltpu.roll(c, half, 1))
    return c * tab[:, 0:128] + swapped * tab[:, 128:256]


def _q_up_kernel(qn_ref, wn_ref, wr_ref, tab_ref, q_ref):
    scale = (QK_NOPE + QK_ROPE) ** -0.5 * math.log2(math.e)
    lane = lax.broadcasted_iota(jnp.int32, (ROW_SLAB, 128), 1)
    for rows in _row_slabs(qn_ref.shape[0]):
        qn = qn_ref[rows, :]
        tab = tab_ref[rows, :]
        y_n = _dot(qn, wn_ref[...])
        y_r = _dot(qn, wr_ref[...])
        for pair in range(H_B // 2):
            rope = _rope_lane_pairs(y_r[:, pair * 128:(pair + 1) * 128], tab) * scale
            for h, keep in ((2 * pair, lane < QK_ROPE), (2 * pair + 1, lane >= QK_ROPE)):
                q_ref[0, h, rows, 0:QK_NOPE] = (y_n[:, h * QK_NOPE:(h + 1) * QK_NOPE] * scale).astype(BF16)
                q_ref[0, h, rows, QK_NOPE:] = jnp.where(keep, rope, 0.0).astype(BF16)


def _q_up(qn, wn, wr, tab, batch, seq):
    tm = WIDE_ROW_TILE
    nseq = seq // tm
    return pl.pallas_call(
        _q_up_kernel,
        grid=(batch, nseq),
        in_specs=[pl.BlockSpec((tm, Q_LORA), lambda b, i: (b * nseq + i, 0)),
                  _resident(wn.shape), _resident(wr.shape),
                  pl.BlockSpec((tm, 256), lambda b, i: (i, 0))],
        out_specs=pl.BlockSpec((1, H_B, tm, QK_PAD), lambda b, i: (b, 0, i, 0)),
        out_shape=jax.ShapeDtypeStruct((batch, H_B, seq, QK_PAD), BF16),
        compiler_params=_params("arbitrary", "arbitrary"),
        name="q_up",
    )(qn, wn, wr, tab)


def _kv_up_kernel(lat_ref, kr_ref, wk_ref, wvt_ref, k_ref, vt_ref):
    tk = ATTN_TK
    for blk in range(lat_ref.shape[0] // tk):
        rows = slice(blk * tk, (blk + 1) * tk)
        lat = lat_ref[rows, :]
        kn = _dot(lat, wk_ref[...]).astype(BF16)
        kr_even = kr_ref[rows, :]
        kr_odd = pltpu.roll(kr_even.astype(F32), QK_ROPE, 1).astype(BF16)
        for h in range(H_B):
            k_ref[0, h, rows, 0:QK_NOPE] = kn[:, h * QK_NOPE:(h + 1) * QK_NOPE]
            k_ref[0, h, rows, QK_NOPE:] = kr_even if h % 2 == 0 else kr_odd
        vt = _dot_nt(wvt_ref[...], lat).astype(BF16)
        vt_ref[0, :, blk, 0:V_HEAD, :] = vt.reshape(H_B, V_HEAD, tk)
        extra = lax.broadcasted_iota(jnp.int32, (H_B, V_ROWS - V_HEAD, tk), 1) == 0
        vt_ref[0, :, blk, V_HEAD:, :] = extra.astype(BF16)


def _kv_up(lat, kr, wk, wvt, batch, seq):
    tk = ATTN_TK
    tm = WIDE_ROW_TILE
    nk = seq // tk
    nstep = seq // tm
    row = lambda b, i: (b * nstep + i, 0)
    return pl.pallas_call(
        _kv_up_kernel,
        grid=(batch, nstep),
        in_specs=[pl.BlockSpec((tm, KV_LORA), row), pl.BlockSpec((tm, 2 * QK_ROPE), row),
                  _resident(wk.shape), _resident(wvt.shape)],
        out_specs=[pl.BlockSpec((1, H_B, tm, QK_PAD), lambda b, i: (b, 0, i, 0)),
                   pl.BlockSpec((1, H_B, tm // tk, V_ROWS, tk), lambda b, i: (b, 0, i, 0, 0))],
        out_shape=[jax.ShapeDtypeStruct((batch, H_B, seq, QK_PAD), BF16),
                   jax.ShapeDtypeStruct((batch, H_B, nk, V_ROWS, tk), BF16)],
        compiler_params=_params("arbitrary", "arbitrary"),
        name="kv_up",
    )(lat, kr, wk, wvt)


def _attn_kernel(q_ref, k_ref, vt_ref, sg_ref, o_ref, sa_ref, sb_ref, mxa_ref, mxb_ref, m_ref, acc_ref):
    r = pl.program_id(2)
    last = pl.num_programs(2) - 1
    tq = ATTN_TQ
    tk = ATTN_TK

    def q_rows(pair, row):
        start = pl.multiple_of((2 * pair + row) * tq, tq)
        return q_ref[0, 0, pl.ds(start, tq), :]

    def scores(q, j, nblk, s_ref, mx_ref, off=0):
        kb = k_ref[0, 0, pl.ds(pl.multiple_of(j * tk, tk), nblk * tk), :]
        s = _dot_nt(kb, q)
        s_ref[off:off + nblk * tk, :] = s
        mx_ref[...] = jnp.max(s, axis=0, keepdims=True)

    def update(row, j, nblk, s_ref, mx_ref, diag_at=None, off=0):
        s = s_ref[off:off + nblk * tk, :]
        if diag_at is None:
            block_max = mx_ref[...]
        else:
            kpos = lax.broadcasted_iota(jnp.int32, s.shape, 0)
            qpos = lax.broadcasted_iota(jnp.int32, s.shape, 1)
            s = jnp.where(kpos <= qpos + diag_at, s, MASK_VALUE)
            block_max = jnp.max(s, axis=0, keepdims=True)
        m = m_ref[row]
        m_new = jnp.maximum(m, block_max)
        alpha = jnp.exp2(m - m_new)
        p = jnp.exp2(s - m_new).astype(BF16)
        m_ref[row] = m_new
        pv = _dot(vt_ref[0, 0, j], p[0:tk])
        for i in range(1, nblk):
            pv += _dot(vt_ref[0, 0, j + i], p[i * tk:(i + 1) * tk])
        acc_ref[row] = alpha * acc_ref[row] + pv

    def finalize(row):
        o = (acc_ref[row, 0:V_HEAD, :] / acc_ref[row, V_HEAD:V_HEAD + 1, :]).T
        rows = slice(row * tq, (row + 1) * tq)
        o_ref[0, rows, :] = (o * sg_ref[0, rows, :].astype(F32)).astype(BF16)

    m_ref[...] = jnp.full_like(m_ref, MASK_VALUE)
    acc_ref[...] = jnp.zeros_like(acc_ref)

    @pl.when(r == 0)
    def _():
        scores(q_rows(r, 0), 0, 1, sa_ref, mxa_ref, off=tk)

    @pl.when(r > 0)
    def _():
        def step(t):
            j = 2 * t
            scores(q_rows(r, 1), j, 2, sb_ref, mxb_ref)
            update(0, j, 2, sa_ref, mxa_ref)
            scores(q_rows(r, 0), jnp.minimum(j + 2, 2 * r - 1), 2, sa_ref, mxa_ref)
            update(1, j, 2, sb_ref, mxb_ref)

        def body(i, carry):
            step(2 * i)
            step(2 * i + 1)
            return carry

        lax.fori_loop(0, r // 2, body, 0)

        @pl.when(r % 2 == 1)
        def _():
            step(r - 1)

    scores(q_rows(r, 1), 2 * r, 2, sb_ref, mxb_ref)
    update(0, 2 * r, 1, sa_ref, mxa_ref, diag_at=0, off=tk)
    scores(q_rows(jnp.minimum(r + 1, last), 0), 0, 2, sa_ref, mxa_ref)
    update(1, 2 * r, 2, sb_ref, mxb_ref, diag_at=tk)
    finalize(0)
    finalize(1)


def _attention(q, k, vt, sg3):
    b, h, s, _ = q.shape
    tq = ATTN_TQ
    assert tq == ATTN_TK
    nk = s // ATTN_TK
    head = lambda bi, hi, ri: (bi, hi, 0, 0)
    oblk = lambda bi, hi, ri: (bi, ri, hi)
    return pl.pallas_call(
        _attn_kernel,
        grid=(b, h, s // (2 * tq)),
        in_specs=[pl.BlockSpec((1, 1, s, QK_PAD), head),
                  pl.BlockSpec((1, 1, s, QK_PAD), head),
                  pl.BlockSpec((1, 1, nk, V_ROWS, ATTN_TK), lambda bi, hi, ri: (bi, hi, 0, 0, 0)),
                  pl.BlockSpec((1, 2 * tq, V_HEAD), oblk)],
        out_specs=pl.BlockSpec((1, 2 * tq, V_HEAD), oblk),
        out_shape=jax.ShapeDtypeStruct((b, s, WIDTH_B), BF16),
        scratch_shapes=[pltpu.VMEM((2 * ATTN_TK, tq), F32), pltpu.VMEM((2 * ATTN_TK, tq), F32),
                        pltpu.VMEM((1, tq), F32), pltpu.VMEM((1, tq), F32),
                        pltpu.VMEM((2, 1, tq), F32), pltpu.VMEM((2, V_ROWS, tq), F32)],
        compiler_params=_params("arbitrary", "arbitrary", "arbitrary"),
        name="mla_attention",
    )(q, k, vt, sg3)


def _rope_angles(seq, half, base):
    inv = base ** (-jnp.arange(half, dtype=F32) / half)
    return jnp.arange(seq, dtype=F32)[:, None] * inv[None, :]


def _retention_decays(c, rows):
    lg = jnp.log1p(-jnp.exp2(-5.0 - jnp.arange(H_A, dtype=F32)))
    idx = jnp.arange(c, dtype=F32)
    up = jnp.exp((idx + 1.0)[:, None] * lg[None, :])
    down = jnp.exp(-(idx + 1.0)[:, None] * lg[None, :]) * DK_A ** -0.5
    widen = lambda t: jnp.tile(jnp.repeat(t, DK_A, axis=1), (rows // c, 1))
    causal = (idx[:, None] >= idx[None, :]).astype(F32)
    cdec = jnp.broadcast_to(jnp.exp(c * lg)[:, None, None], (H_A, 1, DV_A))
    return widen(up), widen(down), causal, cdec


def _swap_halves(w):
    half = w.shape[-1] // 2
    return jnp.concatenate([w[..., half:], w[..., :half]], axis=-1)


def kernel(x, a_w_in, a_w_out, b_w_in, b_q_norm, b_w_uq, b_w_out, kv_w_down, kv_norm, kv_w_up,
           ln_g, ln_b):
    batch, seq, _ = x.shape
    t = batch * seq
    x2 = x.reshape(t, D_MODEL)

    ang_a = _rope_angles(seq, DK_A // 2, ROPE_BASE_A)
    cos_a, sin_a = jnp.cos(ang_a), jnp.sin(ang_a)
    ang_b = _rope_angles(seq, QK_ROPE // 2, ROPE_BASE_B)
    cos_b, sin_b = jnp.cos(ang_b), jnp.sin(ang_b)
    tab_b = jnp.concatenate([cos_b, cos_b, -sin_b, sin_b], axis=-1)

    qdec, kdec, causal, cdec = _retention_decays(RET_CHUNK, ROW_TILE)
    q, k, v, sg = _ret_inproj(x2, a_w_in[0].astype(BF16), cos_a, sin_a, qdec, kdec, seq)
    shape3 = lambda a: a.reshape(batch, seq, a.shape[-1])
    o = _retention(shape3(q), shape3(k), shape3(v), shape3(sg), causal, cdec)
    x1 = _outproj_ln(o.reshape(t, WIDTH_A), a_w_out[0].astype(BF16), x2,
                     ln_g[0][None, :], ln_b[0][None, :])

    w_rope = kv_w_down[:, KV_LORA:]
    wdn = jnp.concatenate([kv_w_down[:, :KV_LORA], w_rope, _swap_halves(w_rope)], axis=-1)
    qn, sgb, lat, kr = _mla_inproj(x1, b_w_in[0].astype(BF16), wdn.astype(BF16),
                                   b_q_norm[0][None, :], kv_norm[None, :], tab_b, seq)

    wuq = b_w_uq[0].reshape(Q_LORA, H_B, QK_NOPE + QK_ROPE)
    wuq_n = wuq[..., :QK_NOPE].reshape(Q_LORA, H_B * QK_NOPE).astype(BF16)
    wuq_r = wuq[..., QK_NOPE:].reshape(Q_LORA, H_B * QK_ROPE).astype(BF16)
    tab_q = jnp.concatenate([cos_b] * 4 + [-sin_b, sin_b] * 2, axis=-1)
    qh = _q_up(qn, wuq_n, wuq_r, tab_q, batch, seq)

    wup = kv_w_up.reshape(KV_LORA, H_B, QK_NOPE + V_HEAD)
    wk = wup[..., :QK_NOPE].reshape(KV_LORA, H_B * QK_NOPE).astype(BF16)
    wvt = wup[..., QK_NOPE:].reshape(KV_LORA, H_B * V_HEAD).T.astype(BF16)
    kh, vt = _kv_up(lat, kr, wk, wvt, batch, seq)

    ob = _attention(qh, kh, vt, sgb.reshape(batch, seq, WIDTH_B))
    out = _outproj_ln(ob.reshape(t, WIDTH_B), b_w_out[0].astype(BF16), x1,
                      ln_g[1][None, :], ln_b[1][None, :])
    return out.reshape(batch, seq, D_MODEL)
```

```python
import functools
import math

import jax
import jax.numpy as jnp
from jax import lax
from jax.experimental import pallas as pl
from jax.experimental.pallas import tpu as pltpu

D_MODEL = 1024
DEPTH = 2

H_A = 4
DK_A = D_MODEL // H_A
DV_A = 2 * DK_A
WIDTH_A = H_A * DV_A
ROPE_BASE_A = 10000.0

H_B = 16
QK_NOPE = 128
QK_ROPE = 64
V_HEAD = 128
Q_LORA = 768
KV_LORA = 512
WIDTH_B = H_B * V_HEAD
ROPE_BASE_B = 10000.0
QK_PAD = 256
V_ROWS = V_HEAD + 16

DEEPNORM_ALPHA = (2.0 * DEPTH) ** 0.25

RET_CHUNK = 256
ROW_TILE = 512
WIDE_ROW_TILE = 1024
ROW_SLAB = 256
ATTN_TQ = 512
ATTN_TK = 512
MASK_VALUE = -1e30

VMEM_LIMIT_BYTES = 56 * 1024 * 1024

BF16 = jnp.bfloat16
F32 = jnp.float32


def _params(*semantics):
    return pltpu.CompilerParams(dimension_semantics=semantics,
                                vmem_limit_bytes=VMEM_LIMIT_BYTES)


def _resident(shape):
    zeros = (0,) * len(shape)
    return pl.BlockSpec(shape, lambda *_: zeros, pipeline_mode=pl.Buffered(1))


def _dot(a, b):
    return jnp.dot(a, b, preferred_element_type=F32)


def _dot_nt(a, b):
    return lax.dot_general(a, b, (((1,), (1,)), ((), ())), preferred_element_type=F32)


def _dot_tn(a, b):
    return lax.dot_general(a, b, (((0,), (0,)), ((), ())), preferred_element_type=F32)


def _row_slabs(rows):
    return [slice(i, i + ROW_SLAB) for i in range(0, rows, ROW_SLAB)]


def _silu(g):
    return g / (1.0 + jnp.exp(-g))


def _ret_inproj_kernel(x_ref, w_ref, cos_ref, sin_ref, qdec_ref, kdec_ref, q_ref, k_ref, v_ref, sg_ref):
    qk = H_A * DK_A
    half = DK_A // 2

    def rope_store(h, out_ref, rows, cos, sin, dec_ref):
        for i in range(H_A):
            lo = slice(i * DK_A, i * DK_A + half)
            hi = slice(i * DK_A + half, (i + 1) * DK_A)
            x1 = h[:, lo]
            x2 = h[:, hi]
            out_ref[rows, lo] = ((x1 * cos - x2 * sin) * dec_ref[rows, lo]).astype(BF16)
            out_ref[rows, hi] = ((x2 * cos + x1 * sin) * dec_ref[rows, hi]).astype(BF16)

    for rows in _row_slabs(x_ref.shape[0]):
        xb = x_ref[rows, :].astype(BF16)
        cos = cos_ref[rows, :]
        sin = sin_ref[rows, :]
        rope_store(_dot(xb, w_ref[:, 0:qk]), q_ref, rows, cos, sin, qdec_ref)
        rope_store(_dot(xb, w_ref[:, qk:2 * qk]), k_ref, rows, cos, sin, kdec_ref)
        v_ref[rows, :] = _dot(xb, w_ref[:, 2 * qk:2 * qk + WIDTH_A]).astype(BF16)
        sg_ref[rows, :] = _silu(_dot(xb, w_ref[:, 2 * qk + WIDTH_A:])).astype(BF16)


def _ret_inproj(x2, w, cos, sin, qdec, kdec, seq):
    t = x2.shape[0]
    tm = ROW_TILE
    nseq = seq // tm
    qk = H_A * DK_A
    row = lambda i: (i, 0)
    pos = lambda i: (i % nseq, 0)
    return pl.pallas_call(
        _ret_inproj_kernel,
        grid=(t // tm,),
        in_specs=[pl.BlockSpec((tm, D_MODEL), row),
                  _resident(w.shape),
                  pl.BlockSpec((tm, DK_A // 2), pos),
                  pl.BlockSpec((tm, DK_A // 2), pos),
                  _resident(qdec.shape), _resident(kdec.shape)],
        out_specs=[pl.BlockSpec((tm, qk), row), pl.BlockSpec((tm, qk), row),
                   pl.BlockSpec((tm, WIDTH_A), row), pl.BlockSpec((tm, WIDTH_A), row)],
        out_shape=[jax.ShapeDtypeStruct((t, qk), BF16), jax.ShapeDtypeStruct((t, qk), BF16),
                   jax.ShapeDtypeStruct((t, WIDTH_A), BF16), jax.ShapeDtypeStruct((t, WIDTH_A), BF16)],
        compiler_params=_params("arbitrary"),
        name="ret_inproj",
    )(x2, w, cos, sin, qdec, kdec)


def _retention_kernel(q_ref, k_ref, v_ref, sg_ref, causal_ref, cdec_ref, o_ref, state_ref, sbf_ref):
    @pl.when(pl.program_id(1) == 0)
    def _():
        state_ref[...] = jnp.zeros_like(state_ref)
        sbf_ref[...] = jnp.zeros_like(sbf_ref)

    for h in range(H_A):
        qk_cols = slice(h * DK_A, (h + 1) * DK_A)
        v_cols = slice(h * DV_A, (h + 1) * DV_A)
        q = q_ref[0, :, qk_cols]
        k = k_ref[0, :, qk_cols]
        v = v_ref[0, :, v_cols]
        scores = (_dot_nt(q, k) * causal_ref[...]).astype(BF16)
        o = _dot(scores, v) + _dot(q, sbf_ref[h])
        state = (state_ref[h] + _dot_tn(k, v)) * cdec_ref[h]
        state_ref[h] = state
        sbf_ref[h] = state.astype(BF16)

        mu = jnp.mean(o, axis=-1, keepdims=True)
        d = o - mu
        var = jnp.mean(d * d, axis=-1, keepdims=True)
        gated = d * lax.rsqrt(var + 1e-5) * sg_ref[0, :, v_cols].astype(F32)
        o_ref[0, :, v_cols] = gated.astype(BF16)


def _retention(q, k, v, sg, causal, cdec):
    b, s, _ = q.shape
    c = RET_CHUNK
    tok = lambda bi, ci: (bi, ci, 0)
    return pl.pallas_call(
        _retention_kernel,
        grid=(b, s // c),
        in_specs=[pl.BlockSpec((1, c, H_A * DK_A), tok), pl.BlockSpec((1, c, H_A * DK_A), tok),
                  pl.BlockSpec((1, c, WIDTH_A), tok), pl.BlockSpec((1, c, WIDTH_A), tok),
                  _resident(causal.shape), _resident(cdec.shape)],
        out_specs=pl.BlockSpec((1, c, WIDTH_A), tok),
        out_shape=jax.ShapeDtypeStruct((b, s, WIDTH_A), BF16),
        scratch_shapes=[pltpu.VMEM((H_A, DK_A, DV_A), F32), pltpu.VMEM((H_A, DK_A, DV_A), BF16)],
        compiler_params=_params("arbitrary", "arbitrary"),
        name="retention",
    )(q, k, v, sg, causal, cdec)


def _outproj_ln_kernel(a_ref, w_ref, x_ref, g_ref, b_ref, o_ref):
    for rows in _row_slabs(a_ref.shape[0]):
        z = DEEPNORM_ALPHA * x_ref[rows, :] + _dot(a_ref[rows, :], w_ref[...])
        mu = jnp.mean(z, axis=-1, keepdims=True)
        d = z - mu
        var = jnp.mean(d * d, axis=-1, keepdims=True)
        o_ref[rows, :] = d * lax.rsqrt(var + 1e-5) * g_ref[...] + b_ref[...]


def _outproj_ln(a, w, x2, g, bias):
    t, width = a.shape
    tm = WIDE_ROW_TILE
    row = lambda i: (i, 0)
    return pl.pallas_call(
        _outproj_ln_kernel,
        grid=(t // tm,),
        in_specs=[pl.BlockSpec((tm, width), row), _resident(w.shape),
                  pl.BlockSpec((tm, D_MODEL), row), _resident(g.shape), _resident(bias.shape)],
        out_specs=pl.BlockSpec((tm, D_MODEL), row),
        out_shape=jax.ShapeDtypeStruct((t, D_MODEL), F32),
        compiler_params=_params("arbitrary"),
        name="outproj_ln",
    )(a, w, x2, g, bias)


def _rope_pair(u, tab):
    t = u * tab
    lane = lax.broadcasted_iota(jnp.int32, t.shape, 1)
    return jnp.where(lane < QK_ROPE, t + pltpu.roll(t, QK_ROPE, 1), 0.0)


def _rms(x, g, eps=1e-6):
    return x * lax.rsqrt(jnp.mean(x * x, axis=-1, keepdims=True) + eps) * g


def _mla_inproj_kernel(x_ref, win_ref, wdn_ref, qg_ref, kvg_ref, tab_ref,
                       qn_ref, sg_ref, lat_ref, kr_ref):
    for rows in _row_slabs(x_ref.shape[0]):
        xb = x_ref[rows, :].astype(BF16)
        qn_ref[rows, :] = _rms(_dot(xb, win_ref[:, 0:Q_LORA]), qg_ref[...]).astype(BF16)
        sg_ref[rows, :] = _silu(_dot(xb, win_ref[:, Q_LORA:])).astype(BF16)
        c = _dot(xb, wdn_ref[...])
        lat_ref[rows, :] = _rms(c[:, 0:KV_LORA], kvg_ref[...]).astype(BF16)
        kr_ref[rows, :] = _rope_pair(c[:, KV_LORA:], tab_ref[rows, :]).astype(BF16)


def _mla_inproj(x2, win, wdn, qg, kvg, tab, seq):
    t = x2.shape[0]
    tm = WIDE_ROW_TILE
    nseq = seq // tm
    row = lambda i: (i, 0)
    pos = lambda i: (i % nseq, 0)
    widths = (Q_LORA, WIDTH_B, KV_LORA, 2 * QK_ROPE)
    return pl.pallas_call(
        _mla_inproj_kernel,
        grid=(t // tm,),
        in_specs=[pl.BlockSpec((tm, D_MODEL), row), _resident(win.shape), _resident(wdn.shape),
                  _resident(qg.shape), _resident(kvg.shape),
                  pl.BlockSpec((tm, 2 * QK_ROPE), pos)],
        out_specs=[pl.BlockSpec((tm, w), row) for w in widths],
        out_shape=[jax.ShapeDtypeStruct((t, w), BF16) for w in widths],
        compiler_params=_params("arbitrary"),
        name="mla_inproj",
    )(x2, win, wdn, qg, kvg, tab)


def _rope_lane_pairs(c, tab):
    lane = lax.broadcasted_iota(jnp.int32, c.shape, 1)
    half = QK_ROPE // 2
    swapped = jnp.where(lane % QK_ROPE < half, pltpu.roll(c, 128 - half, 1), pltpu.roll(c, half, 1))
    return c * tab[:, 0:128] + swapped * tab[:, 128:256]


def _q_up_kernel(qn_ref, wn_ref, wr_ref, tab_ref, q_ref):
    scale = (QK_NOPE + QK_ROPE) ** -0.5 * math.log2(math.e)
    lane = lax.broadcasted_iota(jnp.int32, (ROW_SLAB, 128), 1)
    for rows in _row_slabs(qn_ref.shape[0]):
        qn = qn_ref[rows, :]
        tab = tab_ref[rows, :]
        y_n = _dot(qn, wn_ref[...])
        y_r = _dot(qn, wr_ref[...])
        for pair in range(H_B // 2):
            rope = _rope_lane_pairs(y_r[:, pair * 128:(pair + 1) * 128], tab) * scale
            for h, keep in ((2 * pair, lane < QK_ROPE), (2 * pair + 1, lane >= QK_ROPE)):
                q_ref[0, h, rows, 0:QK_NOPE] = (y_n[:, h * QK_NOPE:(h + 1) * QK_NOPE] * scale).astype(BF16)
                q_ref[0, h, rows, QK_NOPE:] = jnp.where(keep, rope, 0.0).astype(BF16)


def _q_up(qn, wn, wr, tab, batch, seq):
    tm = WIDE_ROW_TILE
    nseq = seq // tm
    return pl.pallas_call(
        _q_up_kernel,
        grid=(batch, nseq),
        in_specs=[pl.BlockSpec((tm, Q_LORA), lambda b, i: (b * nseq + i, 0)),
                  _resident(wn.shape), _resident(wr.shape),
                  pl.BlockSpec((tm, 256), lambda b, i: (i, 0))],
        out_specs=pl.BlockSpec((1, H_B, tm, QK_PAD), lambda b, i: (b, 0, i, 0)),
        out_shape=jax.ShapeDtypeStruct((batch, H_B, seq, QK_PAD), BF16),
        compiler_params=_params("arbitrary", "arbitrary"),
        name="q_up",
    )(qn, wn, wr, tab)


def _kv_up_kernel(lat_ref, kr_ref, wk_ref, wvt_ref, k_ref, vt_ref):
    tk = ATTN_TK
    for blk in range(lat_ref.shape[0] // tk):
        rows = slice(blk * tk, (blk + 1) * tk)
        lat = lat_ref[rows, :]
        kn = _dot(lat, wk_ref[...]).astype(BF16)
        kr_even = kr_ref[rows, :]
        kr_odd = pltpu.roll(kr_even.astype(F32), QK_ROPE, 1).astype(BF16)
        for h in range(H_B):
            k_ref[0, h, rows, 0:QK_NOPE] = kn[:, h * QK_NOPE:(h + 1) * QK_NOPE]
            k_ref[0, h, rows, QK_NOPE:] = kr_even if h % 2 == 0 else kr_odd
        vt = _dot_nt(wvt_ref[...], lat).astype(BF16)
        vt_ref[0, :, blk, 0:V_HEAD, :] = vt.reshape(H_B, V_HEAD, tk)
        extra = lax.broadcasted_iota(jnp.int32, (H_B, V_ROWS - V_HEAD, tk), 1) == 0
        vt_ref[0, :, blk, V_HEAD:, :] = extra.astype(BF16)


def _kv_up(lat, kr, wk, wvt, batch, seq):
    tk = ATTN_TK
    tm = WIDE_ROW_TILE
    nk = seq // tk
    nstep = seq // tm
    row = lambda b, i: (b * nstep + i, 0)
    return pl.pallas_call(
        _kv_up_kernel,
        grid=(batch, nstep),
        in_specs=[pl.BlockSpec((tm, KV_LORA), row), pl.BlockSpec((tm, 2 * QK_ROPE), row),
                  _resident(wk.shape), _resident(wvt.shape)],
        out_specs=[pl.BlockSpec((1, H_B, tm, QK_PAD), lambda b, i: (b, 0, i, 0)),
                   pl.BlockSpec((1, H_B, tm // tk, V_ROWS, tk), lambda b, i: (b, 0, i, 0, 0))],
        out_shape=[jax.ShapeDtypeStruct((batch, H_B, seq, QK_PAD), BF16),
                   jax.ShapeDtypeStruct((batch, H_B, nk, V_ROWS, tk), BF16)],
        compiler_params=_params("arbitrary", "arbitrary"),
        name="kv_up",
    )(lat, kr, wk, wvt)


def _attn_kernel(q_ref, k_ref, vt_ref, sg_ref, o_ref, sa_ref, sb_ref, mxa_ref, mxb_ref, m_ref, acc_ref):
    tq = ATTN_TQ
    tk = ATTN_TK
    npairs = q_ref.shape[2] // (2 * tq)
    last = npairs - 1

    def q_rows(pair, row):
        start = pl.multiple_of((2 * pair + row) * tq, tq)
        return q_ref[0, 0, pl.ds(start, tq), :]

    def scores(q, j, nblk, s_ref, mx_ref, off=0):
        kb = k_ref[0, 0, pl.ds(pl.multiple_of(j * tk, tk), nblk * tk), :]
        s = _dot_nt(kb, q)
        s_ref[off:off + nblk * tk, :] = s
        mx_ref[...] = jnp.max(s, axis=0, keepdims=True)

    def update(row, j, nblk, s_ref, mx_ref, diag_at=None, off=0):
        s = s_ref[off:off + nblk * tk, :]
        if diag_at is None:
            block_max = mx_ref[...]
        else:
            kpos = lax.broadcasted_iota(jnp.int32, s.shape, 0)
            qpos = lax.broadcasted_iota(jnp.int32, s.shape, 1)
            s = jnp.where(kpos <= qpos + diag_at, s, MASK_VALUE)
            block_max = jnp.max(s, axis=0, keepdims=True)
        m = m_ref[row]
        m_new = jnp.maximum(m, block_max)
        alpha = jnp.exp2(m - m_new)
        p = jnp.exp2(s - m_new).astype(BF16)
        m_ref[row] = m_new
        pv = _dot(vt_ref[0, 0, j], p[0:tk])
        for i in range(1, nblk):
            pv += _dot(vt_ref[0, 0, j + i], p[i * tk:(i + 1) * tk])
        acc_ref[row] = alpha * acc_ref[row] + pv

    def finalize(pair, row):
        o = (acc_ref[row, 0:V_HEAD, :] / acc_ref[row, V_HEAD:V_HEAD + 1, :]).T
        rows = pl.ds(pl.multiple_of((2 * pair + row) * tq, tq), tq)
        o_ref[0, rows, :] = (o * sg_ref[0, rows, :].astype(F32)).astype(BF16)

    def row_pair(r, carry):
        m_ref[...] = jnp.full_like(m_ref, MASK_VALUE)
        acc_ref[...] = jnp.zeros_like(acc_ref)

        @pl.when(r == 0)
        def _():
            scores(q_rows(r, 0), 0, 1, sa_ref, mxa_ref, off=tk)

        @pl.when(r > 0)
        def _():
            def step(t):
                j = 2 * t
                scores(q_rows(r, 1), j, 2, sb_ref, mxb_ref)
                update(0, j, 2, sa_ref, mxa_ref)
                scores(q_rows(r, 0), jnp.minimum(j + 2, 2 * r - 1), 2, sa_ref, mxa_ref)
                update(1, j, 2, sb_ref, mxb_ref)

            def body(i, carry):
                step(2 * i)
                step(2 * i + 1)
                return carry

            lax.fori_loop(0, r // 2, body, 0)

            @pl.when(r % 2 == 1)
            def _():
                step(r - 1)

        scores(q_rows(r, 1), 2 * r, 2, sb_ref, mxb_ref)
        update(0, 2 * r, 1, sa_ref, mxa_ref, diag_at=0, off=tk)
        scores(q_rows(jnp.minimum(r + 1, last), 0), 0, 2, sa_ref, mxa_ref)
        update(1, 2 * r, 2, sb_ref, mxb_ref, diag_at=tk)
        finalize(r, 0)
        finalize(r, 1)
        return carry

    lax.fori_loop(0, npairs, row_pair, 0)


def _attention(q, k, vt, sg3):
    b, h, s, _ = q.shape
    tq = ATTN_TQ
    assert tq == ATTN_TK
    nk = s // ATTN_TK
    head = lambda bi, hi: (bi, hi, 0, 0)
    oblk = lambda bi, hi: (bi, 0, hi)
    return pl.pallas_call(
        _attn_kernel,
        grid=(b, h),
        in_specs=[pl.BlockSpec((1, 1, s, QK_PAD), head),
                  pl.BlockSpec((1, 1, s, QK_PAD), head),
                  pl.BlockSpec((1, 1, nk, V_ROWS, ATTN_TK), lambda bi, hi: (bi, hi, 0, 0, 0)),
                  pl.BlockSpec((1, s, V_HEAD), oblk)],
        out_specs=pl.BlockSpec((1, s, V_HEAD), oblk),
        out_shape=jax.ShapeDtypeStruct((b, s, WIDTH_B), BF16),
        scratch_shapes=[pltpu.VMEM((2 * ATTN_TK, tq), F32), pltpu.VMEM((2 * ATTN_TK, tq), F32),
                        pltpu.VMEM((1, tq), F32), pltpu.VMEM((1, tq), F32),
                        pltpu.VMEM((2, 1, tq), F32), pltpu.VMEM((2, V_ROWS, tq), F32)],
        compiler_params=_params("arbitrary", "arbitrary"),
        name="mla_attention",
    )(q, k, vt, sg3)


def _rope_angles(seq, half, base):
    inv = base ** (-jnp.arange(half, dtype=F32) / half)
    return jnp.arange(seq, dtype=F32)[:, None] * inv[None, :]


def _retention_decays(c, rows):
    lg = jnp.log1p(-jnp.exp2(-5.0 - jnp.arange(H_A, dtype=F32)))
    idx = jnp.arange(c, dtype=F32)
    up = jnp.exp((idx + 1.0)[:, None] * lg[None, :])
    down = jnp.exp(-(idx + 1.0)[:, None] * lg[None, :]) * DK_A ** -0.5
    widen = lambda t: jnp.tile(jnp.repeat(t, DK_A, axis=1), (rows // c, 1))
    causal = (idx[:, None] >= idx[None, :]).astype(F32)
    cdec = jnp.broadcast_to(jnp.exp(c * lg)[:, None, None], (H_A, 1, DV_A))
    return widen(up), widen(down), causal, cdec


def _swap_halves(w):
    half = w.shape[-1] // 2
    return jnp.concatenate([w[..., half:], w[..., :half]], axis=-1)


def kernel(x, a_w_in, a_w_out, b_w_in, b_q_norm, b_w_uq, b_w_out, kv_w_down, kv_norm, kv_w_up,
           ln_g, ln_b):
    batch, seq, _ = x.shape
    t = batch * seq
    x2 = x.reshape(t, D_MODEL)

    ang_a = _rope_angles(seq, DK_A // 2, ROPE_BASE_A)
    cos_a, sin_a = jnp.cos(ang_a), jnp.sin(ang_a)
    ang_b = _rope_angles(seq, QK_ROPE // 2, ROPE_BASE_B)
    cos_b, sin_b = lax.optimization_barrier((jnp.cos(ang_b), jnp.sin(ang_b)))
    tab_b = jnp.concatenate([cos_b, cos_b, -sin_b, sin_b], axis=-1)

    qdec, kdec, causal, cdec = _retention_decays(RET_CHUNK, ROW_TILE)
    q, k, v, sg = _ret_inproj(x2, a_w_in[0].astype(BF16), cos_a, sin_a, qdec, kdec, seq)
    shape3 = lambda a: a.reshape(batch, seq, a.shape[-1])
    o = _retention(shape3(q), shape3(k), shape3(v), shape3(sg), causal, cdec)
    x1 = _outproj_ln(o.reshape(t, WIDTH_A), a_w_out[0].astype(BF16), x2,
                     ln_g[0][None, :], ln_b[0][None, :])

    w_rope = kv_w_down[:, KV_LORA:]
    wdn = jnp.concatenate([kv_w_down[:, :KV_LORA], w_rope, _swap_halves(w_rope)], axis=-1)
    qn, sgb, lat, kr = _mla_inproj(x1, b_w_in[0].astype(BF16), wdn.astype(BF16),
                                   b_q_norm[0][None, :], kv_norm[None, :], tab_b, seq)

    wuq = b_w_uq[0].reshape(Q_LORA, H_B, QK_NOPE + QK_ROPE)
    wuq_n = wuq[..., :QK_NOPE].reshape(Q_LORA, H_B * QK_NOPE).astype(BF16)
    wuq_r = wuq[..., QK_NOPE:].reshape(Q_LORA, H_B * QK_ROPE).astype(BF16)
    tab_q = jnp.concatenate([cos_b] * 4 + [-sin_b, sin_b] * 2, axis=-1)
    qh = _q_up(qn, wuq_n, wuq_r, tab_q, batch, seq)

    wup = kv_w_up.reshape(KV_LORA, H_B, QK_NOPE + V_HEAD)
    wk = wup[..., :QK_NOPE].reshape(KV_LORA, H_B * QK_NOPE).astype(BF16)
    wvt = wup[..., QK_NOPE:].reshape(KV_LORA, H_B * V_HEAD).T.astype(BF16)
    kh, vt = _kv_up(lat, kr, wk, wvt, batch, seq)

    ob = _attention(qh, kh, vt, sgb.reshape(batch, seq, WIDTH_B))
    out = _outproj_ln(ob.reshape(t, WIDTH_B), b_w_out[0].astype(BF16), x1,
                      ln_g[1][None, :], ln_b[1][None, :])
    return out.reshape(batch, seq, D_MODEL)
```

```python
import functools
import math

import jax
import jax.numpy as jnp
from jax import lax
from jax.experimental import pallas as pl
from jax.experimental.pallas import tpu as pltpu

D_MODEL = 1024
DEPTH = 2

H_A = 4
DK_A = D_MODEL // H_A
DV_A = 2 * DK_A
WIDTH_A = H_A * DV_A
ROPE_BASE_A = 10000.0

H_B = 16
QK_NOPE = 128
QK_ROPE = 64
V_HEAD = 128
Q_LORA = 768
KV_LORA = 512
WIDTH_B = H_B * V_HEAD
ROPE_BASE_B = 10000.0
QK_PAD = 256
V_ROWS = V_HEAD + 16

DEEPNORM_ALPHA = (2.0 * DEPTH) ** 0.25

RET_CHUNK = 256
ROW_TILE = 512
WIDE_ROW_TILE = 1024
ROW_SLAB = 256
ATTN_TQ = 512
ATTN_TK = 512
MASK_VALUE = -1e30

VMEM_LIMIT_BYTES = 56 * 1024 * 1024

BF16 = jnp.bfloat16
F32 = jnp.float32


def _params(*semantics):
    return pltpu.CompilerParams(dimension_semantics=semantics,
                                vmem_limit_bytes=VMEM_LIMIT_BYTES)


def _resident(shape):
    zeros = (0,) * len(shape)
    return pl.BlockSpec(shape, lambda *_: zeros, pipeline_mode=pl.Buffered(1))


def _dot(a, b):
    return jnp.dot(a, b, preferred_element_type=F32)


def _dot_nt(a, b):
    return lax.dot_general(a, b, (((1,), (1,)), ((), ())), preferred_element_type=F32)


def _dot_tn(a, b):
    return lax.dot_general(a, b, (((0,), (0,)), ((), ())), preferred_element_type=F32)


def _row_slabs(rows):
    return [slice(i, i + ROW_SLAB) for i in range(0, rows, ROW_SLAB)]


def _silu(g):
    return g / (1.0 + jnp.exp(-g))


def _ret_inproj_kernel(x_ref, w_ref, cos_ref, sin_ref, qdec_ref, kdec_ref, q_ref, k_ref, v_ref, sg_ref):
    qk = H_A * DK_A
    half = DK_A // 2

    def rope_store(h, out_ref, rows, cos, sin, dec_ref):
        for i in range(H_A):
            lo = slice(i * DK_A, i * DK_A + half)
            hi = slice(i * DK_A + half, (i + 1) * DK_A)
            x1 = h[:, lo]
            x2 = h[:, hi]
            out_ref[rows, lo] = ((x1 * cos - x2 * sin) * dec_ref[rows, lo]).astype(BF16)
            out_ref[rows, hi] = ((x2 * cos + x1 * sin) * dec_ref[rows, hi]).astype(BF16)

    for rows in _row_slabs(x_ref.shape[0]):
        xb = x_ref[rows, :].astype(BF16)
        cos = cos_ref[rows, :]
        sin = sin_ref[rows, :]
        rope_store(_dot(xb, w_ref[:, 0:qk]), q_ref, rows, cos, sin, qdec_ref)
        rope_store(_dot(xb, w_ref[:, qk:2 * qk]), k_ref, rows, cos, sin, kdec_ref)
        v_ref[rows, :] = _dot(xb, w_ref[:, 2 * qk:2 * qk + WIDTH_A]).astype(BF16)
        sg_ref[rows, :] = _silu(_dot(xb, w_ref[:, 2 * qk + WIDTH_A:])).astype(BF16)


def _ret_inproj(x2, w, cos, sin, qdec, kdec, seq):
    t = x2.shape[0]
    tm = ROW_TILE
    nseq = seq // tm
    qk = H_A * DK_A
    row = lambda i: (i, 0)
    pos = lambda i: (i % nseq, 0)
    return pl.pallas_call(
        _ret_inproj_kernel,
        grid=(t // tm,),
        in_specs=[pl.BlockSpec((tm, D_MODEL), row),
                  _resident(w.shape),
                  pl.BlockSpec((tm, DK_A // 2), pos),
                  pl.BlockSpec((tm, DK_A // 2), pos),
                  _resident(qdec.shape), _resident(kdec.shape)],
        out_specs=[pl.BlockSpec((tm, qk), row), pl.BlockSpec((tm, qk), row),
                   pl.BlockSpec((tm, WIDTH_A), row), pl.BlockSpec((tm, WIDTH_A), row)],
        out_shape=[jax.ShapeDtypeStruct((t, qk), BF16), jax.ShapeDtypeStruct((t, qk), BF16),
                   jax.ShapeDtypeStruct((t, WIDTH_A), BF16), jax.ShapeDtypeStruct((t, WIDTH_A), BF16)],
        compiler_params=_params("arbitrary"),
        name="ret_inproj",
    )(x2, w, cos, sin, qdec, kdec)


def _retention_kernel(q_ref, k_ref, v_ref, sg_ref, causal_ref, cdec_ref, o_ref, state_ref, sbf_ref):
    @pl.when(pl.program_id(1) == 0)
    def _():
        state_ref[...] = jnp.zeros_like(state_ref)
        sbf_ref[...] = jnp.zeros_like(sbf_ref)

    for h in range(H_A):
        qk_cols = slice(h * DK_A, (h + 1) * DK_A)
        v_cols = slice(h * DV_A, (h + 1) * DV_A)
        q = q_ref[0, :, qk_cols]
        k = k_ref[0, :, qk_cols]
        v = v_ref[0, :, v_cols]
        scores = (_dot_nt(q, k) * causal_ref[...]).astype(BF16)
        o = _dot(scores, v) + _dot(q, sbf_ref[h])
        state = (state_ref[h] + _dot_tn(k, v)) * cdec_ref[h]
        state_ref[h] = state
        sbf_ref[h] = state.astype(BF16)

        mu = jnp.mean(o, axis=-1, keepdims=True)
        d = o - mu
        var = jnp.mean(d * d, axis=-1, keepdims=True)
        gated = d * lax.rsqrt(var + 1e-5) * sg_ref[0, :, v_cols].astype(F32)
        o_ref[0, :, v_cols] = gated.astype(BF16)


def _retention(q, k, v, sg, causal, cdec):
    b, s, _ = q.shape
    c = RET_CHUNK
    tok = lambda bi, ci: (bi, ci, 0)
    return pl.pallas_call(
        _retention_kernel,
        grid=(b, s // c),
        in_specs=[pl.BlockSpec((1, c, H_A * DK_A), tok), pl.BlockSpec((1, c, H_A * DK_A), tok),
                  pl.BlockSpec((1, c, WIDTH_A), tok), pl.BlockSpec((1, c, WIDTH_A), tok),
                  _resident(causal.shape), _resident(cdec.shape)],
        out_specs=pl.BlockSpec((1, c, WIDTH_A), tok),
        out_shape=jax.ShapeDtypeStruct((b, s, WIDTH_A), BF16),
        scratch_shapes=[pltpu.VMEM((H_A, DK_A, DV_A), F32), pltpu.VMEM((H_A, DK_A, DV_A), BF16)],
        compiler_params=_params("arbitrary", "arbitrary"),
        name="retention",
    )(q, k, v, sg, causal, cdec)


def _outproj_ln_kernel(a_ref, w_ref, x_ref, g_ref, b_ref, o_ref):
    for rows in _row_slabs(a_ref.shape[0]):
        z = DEEPNORM_ALPHA * x_ref[rows, :] + _dot(a_ref[rows, :], w_ref[...])
        mu = jnp.mean(z, axis=-1, keepdims=True)
        d = z - mu
        var = jnp.mean(d * d, axis=-1, keepdims=True)
        o_ref[rows, :] = d * lax.rsqrt(var + 1e-5) * g_ref[...] + b_ref[...]


def _outproj_ln(a, w, x2, g, bias):
    t, width = a.shape
    tm = WIDE_ROW_TILE
    row = lambda i: (i, 0)
    return pl.pallas_call(
        _outproj_ln_kernel,
        grid=(t // tm,),
        in_specs=[pl.BlockSpec((tm, width), row), _resident(w.shape),
                  pl.BlockSpec((tm, D_MODEL), row), _resident(g.shape), _resident(bias.shape)],
        out_specs=pl.BlockSpec((tm, D_MODEL), row),
        out_shape=jax.ShapeDtypeStruct((t, D_MODEL), F32),
        compiler_params=_params("arbitrary"),
        name="outproj_ln",
    )(a, w, x2, g, bias)


def _rope_pair(u, tab):
    lane = lax.broadcasted_iota(jnp.int32, u.shape, 1)
    t = u * jnp.where(lane < QK_ROPE, tab[:, 0:128], tab[:, 128:256])
    return jnp.where(lane < QK_ROPE, t + pltpu.roll(t, QK_ROPE, 1), 0.0)


def _rms(x, g, eps=1e-6):
    return x * lax.rsqrt(jnp.mean(x * x, axis=-1, keepdims=True) + eps) * g


def _mla_inproj_kernel(x_ref, win_ref, wlat_ref, wrope_ref, qg_ref, kvg_ref, tab_ref,
                       qn_ref, sg_ref, lat_ref, kr_ref):
    for rows in _row_slabs(x_ref.shape[0]):
        xb = x_ref[rows, :].astype(BF16)
        qn_ref[rows, :] = _rms(_dot(xb, win_ref[:, 0:Q_LORA]), qg_ref[...]).astype(BF16)
        sg_ref[rows, :] = _silu(_dot(xb, win_ref[:, Q_LORA:])).astype(BF16)
        lat_ref[rows, :] = _rms(_dot(xb, wlat_ref[...]), kvg_ref[...]).astype(BF16)
        kr_ref[rows, :] = _rope_pair(_dot(xb, wrope_ref[...]), tab_ref[rows, :]).astype(BF16)


def _mla_inproj(x2, win, wlat, wrope, qg, kvg, tab, seq):
    t = x2.shape[0]
    tm = WIDE_ROW_TILE
    nseq = seq // tm
    row = lambda i: (i, 0)
    pos = lambda i: (i % nseq, 0)
    widths = (Q_LORA, WIDTH_B, KV_LORA, 2 * QK_ROPE)
    return pl.pallas_call(
        _mla_inproj_kernel,
        grid=(t // tm,),
        in_specs=[pl.BlockSpec((tm, D_MODEL), row), _resident(win.shape), _resident(wlat.shape),
                  _resident(wrope.shape), _resident(qg.shape), _resident(kvg.shape),
                  pl.BlockSpec((tm, tab.shape[1]), pos)],
        out_specs=[pl.BlockSpec((tm, w), row) for w in widths],
        out_shape=[jax.ShapeDtypeStruct((t, w), BF16) for w in widths],
        compiler_params=_params("arbitrary"),
        name="mla_inproj",
    )(x2, win, wlat, wrope, qg, kvg, tab)


def _rope_lane_pairs(c, tab):
    lane = lax.broadcasted_iota(jnp.int32, c.shape, 1)
    half = QK_ROPE // 2
    swapped = jnp.where(lane % QK_ROPE < half, pltpu.roll(c, 128 - half, 1), pltpu.roll(c, half, 1))
    return c * tab[:, 0:128] + swapped * tab[:, 128:256]


def _q_up_kernel(qn_ref, wn_ref, wr_ref, tab_ref, q_ref):
    scale = (QK_NOPE + QK_ROPE) ** -0.5 * math.log2(math.e)
    lane = lax.broadcasted_iota(jnp.int32, (ROW_SLAB, 128), 1)
    for rows in _row_slabs(qn_ref.shape[0]):
        qn = qn_ref[rows, :]
        tab = tab_ref[rows, :]
        y_n = _dot(qn, wn_ref[...])
        y_r = _dot(qn, wr_ref[...])
        for pair in range(H_B // 2):
            rope = _rope_lane_pairs(y_r[:, pair * 128:(pair + 1) * 128], tab) * scale
            for h, keep in ((2 * pair, lane < QK_ROPE), (2 * pair + 1, lane >= QK_ROPE)):
                q_ref[0, h, rows, 0:QK_NOPE] = (y_n[:, h * QK_NOPE:(h + 1) * QK_NOPE] * scale).astype(BF16)
                q_ref[0, h, rows, QK_NOPE:] = jnp.where(keep, rope, 0.0).astype(BF16)


def _q_up(qn, wn, wr, tab, batch, seq):
    tm = WIDE_ROW_TILE
    nseq = seq // tm
    return pl.pallas_call(
        _q_up_kernel,
        grid=(batch, nseq),
        in_specs=[pl.BlockSpec((tm, Q_LORA), lambda b, i: (b * nseq + i, 0)),
                  _resident(wn.shape), _resident(wr.shape),
                  pl.BlockSpec((tm, 256), lambda b, i: (i, 0))],
        out_specs=pl.BlockSpec((1, H_B, tm, QK_PAD), lambda b, i: (b, 0, i, 0)),
        out_shape=jax.ShapeDtypeStruct((batch, H_B, seq, QK_PAD), BF16),
        compiler_params=_params("arbitrary", "arbitrary"),
        name="q_up",
    )(qn, wn, wr, tab)


def _kv_up_kernel(lat_ref, kr_ref, wk_ref, wvt_ref, k_ref, vt_ref):
    tk = ATTN_TK
    for blk in range(lat_ref.shape[0] // tk):
        rows = slice(blk * tk, (blk + 1) * tk)
        lat = lat_ref[rows, :]
        kn = _dot(lat, wk_ref[...]).astype(BF16)
        kr_even = kr_ref[rows, :]
        kr_odd = pltpu.roll(kr_even.astype(F32), QK_ROPE, 1).astype(BF16)
        for h in range(H_B):
            k_ref[0, h, rows, 0:QK_NOPE] = kn[:, h * QK_NOPE:(h + 1) * QK_NOPE]
            k_ref[0, h, rows, QK_NOPE:] = kr_even if h % 2 == 0 else kr_odd
        vt = _dot_nt(wvt_ref[...], lat).astype(BF16)
        vt_ref[0, :, blk, 0:V_HEAD, :] = vt.reshape(H_B, V_HEAD, tk)
        extra = lax.broadcasted_iota(jnp.int32, (H_B, V_ROWS - V_HEAD, tk), 1) == 0
        vt_ref[0, :, blk, V_HEAD:, :] = extra.astype(BF16)


def _kv_up(lat, kr, wk, wvt, batch, seq):
    tk = ATTN_TK
    tm = WIDE_ROW_TILE
    nk = seq // tk
    nstep = seq // tm
    row = lambda b, i: (b * nstep + i, 0)
    return pl.pallas_call(
        _kv_up_kernel,
        grid=(batch, nstep),
        in_specs=[pl.BlockSpec((tm, KV_LORA), row), pl.BlockSpec((tm, 2 * QK_ROPE), row),
                  _resident(wk.shape), _resident(wvt.shape)],
        out_specs=[pl.BlockSpec((1, H_B, tm, QK_PAD), lambda b, i: (b, 0, i, 0)),
                   pl.BlockSpec((1, H_B, tm // tk, V_ROWS, tk), lambda b, i: (b, 0, i, 0, 0))],
        out_shape=[jax.ShapeDtypeStruct((batch, H_B, seq, QK_PAD), BF16),
                   jax.ShapeDtypeStruct((batch, H_B, nk, V_ROWS, tk), BF16)],
        compiler_params=_params("arbitrary", "arbitrary"),
        name="kv_up",
    )(lat, kr, wk, wvt)


def _attn_kernel(q_ref, k_ref, vt_ref, sg_ref, o_ref, sa_ref, sb_ref, mxa_ref, mxb_ref, m_ref, acc_ref):
    tq = ATTN_TQ
    tk = ATTN_TK
    npairs = q_ref.shape[2] // (2 * tq)
    last = npairs - 1

    def q_rows(pair, row):
        start = pl.multiple_of((2 * pair + row) * tq, tq)
        return q_ref[0, 0, pl.ds(start, tq), :]

    def scores(q, j, nblk, s_ref, mx_ref, off=0):
        kb = k_ref[0, 0, pl.ds(pl.multiple_of(j * tk, tk), nblk * tk), :]
        s = _dot_nt(kb, q)
        s_ref[off:off + nblk * tk, :] = s
        mx_ref[...] = jnp.max(s, axis=0, keepdims=True)

    def update(row, j, nblk, s_ref, mx_ref, diag_at=None, off=0):
        s = s_ref[off:off + nblk * tk, :]
        if diag_at is None:
            block_max = mx_ref[...]
        else:
            kpos = lax.broadcasted_iota(jnp.int32, s.shape, 0)
            qpos = lax.broadcasted_iota(jnp.int32, s.shape, 1)
            s = jnp.where(kpos <= qpos + diag_at, s, MASK_VALUE)
            block_max = jnp.max(s, axis=0, keepdims=True)
        m = m_ref[row]
        m_new = jnp.maximum(m, block_max)
        alpha = jnp.exp2(m - m_new)
        p = jnp.exp2(s - m_new).astype(BF16)
        m_ref[row] = m_new
        pv = _dot(vt_ref[0, 0, j], p[0:tk])
        for i in range(1, nblk):
            pv += _dot(vt_ref[0, 0, j + i], p[i * tk:(i + 1) * tk])
        acc_ref[row] = alpha * acc_ref[row] + pv

    def finalize(pair, row):
        o = (acc_ref[row, 0:V_HEAD, :] / acc_ref[row, V_HEAD:V_HEAD + 1, :]).T
        rows = pl.ds(pl.multiple_of((2 * pair + row) * tq, tq), tq)
        o_ref[0, rows, :] = (o * sg_ref[0, rows, :].astype(F32)).astype(BF16)

    def row_pair(r, carry):
        m_ref[...] = jnp.full_like(m_ref, MASK_VALUE)
        acc_ref[...] = jnp.zeros_like(acc_ref)

        @pl.when(r == 0)
        def _():
            scores(q_rows(r, 0), 0, 1, sa_ref, mxa_ref, off=tk)

        @pl.when(r > 0)
        def _():
            def step(t):
                j = 2 * t
                scores(q_rows(r, 1), j, 2, sb_ref, mxb_ref)
                update(0, j, 2, sa_ref, mxa_ref)
                scores(q_rows(r, 0), jnp.minimum(j + 2, 2 * r - 1), 2, sa_ref, mxa_ref)
                update(1, j, 2, sb_ref, mxb_ref)

            def body(i, carry):
                step(2 * i)
                step(2 * i + 1)
                return carry

            lax.fori_loop(0, r // 2, body, 0)

            @pl.when(r % 2 == 1)
            def _():
                step(r - 1)

        scores(q_rows(r, 1), 2 * r, 2, sb_ref, mxb_ref)
        update(0, 2 * r, 1, sa_ref, mxa_ref, diag_at=0, off=tk)
        scores(q_rows(jnp.minimum(r + 1, last), 0), 0, 2, sa_ref, mxa_ref)
        update(1, 2 * r, 2, sb_ref, mxb_ref, diag_at=tk)
        finalize(r, 0)
        finalize(r, 1)
        return carry

    lax.fori_loop(0, npairs, row_pair, 0)


def _attention(q, k, vt, sg3):
    b, h, s, _ = q.shape
    tq = ATTN_TQ
    assert tq == ATTN_TK
    nk = s // ATTN_TK
    head = lambda bi, hi: (bi, hi, 0, 0)
    oblk = lambda bi, hi: (bi, 0, hi)
    return pl.pallas_call(
        _attn_kernel,
        grid=(b, h),
        in_specs=[pl.BlockSpec((1, 1, s, QK_PAD), head),
                  pl.BlockSpec((1, 1, s, QK_PAD), head),
                  pl.BlockSpec((1, 1, nk, V_ROWS, ATTN_TK), lambda bi, hi: (bi, hi, 0, 0, 0)),
                  pl.BlockSpec((1, s, V_HEAD), oblk)],
        out_specs=pl.BlockSpec((1, s, V_HEAD), oblk),
        out_shape=jax.ShapeDtypeStruct((b, s, WIDTH_B), BF16),
        scratch_shapes=[pltpu.VMEM((2 * ATTN_TK, tq), F32), pltpu.VMEM((2 * ATTN_TK, tq), F32),
                        pltpu.VMEM((1, tq), F32), pltpu.VMEM((1, tq), F32),
                        pltpu.VMEM((2, 1, tq), F32), pltpu.VMEM((2, V_ROWS, tq), F32)],
        compiler_params=_params("arbitrary", "arbitrary"),
        name="mla_attention",
    )(q, k, vt, sg3)


def _rope_angles(seq, half, base, copies=1):
    inv = jnp.tile(base ** (-jnp.arange(half, dtype=F32) / half), copies)
    return jnp.arange(seq, dtype=F32)[:, None] * inv[None, :]


def _rope_lane_table(seq):
    half = QK_ROPE // 2
    ang = _rope_angles(seq, half, ROPE_BASE_B, copies=128 // half)
    sign = jnp.where((jnp.arange(128) // half) % 2 == 0, -1.0, 1.0).astype(F32)
    return jnp.concatenate([jnp.cos(ang), jnp.sin(ang) * sign[None, :]], axis=-1)


def _retention_decays(c, rows):
    lg = jnp.log1p(-jnp.exp2(-5.0 - jnp.arange(H_A, dtype=F32)))
    idx = jnp.arange(c, dtype=F32)
    up = jnp.exp((idx + 1.0)[:, None] * lg[None, :])
    down = jnp.exp(-(idx + 1.0)[:, None] * lg[None, :]) * DK_A ** -0.5
    widen = lambda t: jnp.tile(jnp.repeat(t, DK_A, axis=1), (rows // c, 1))
    causal = (idx[:, None] >= idx[None, :]).astype(F32)
    cdec = jnp.broadcast_to(jnp.exp(c * lg)[:, None, None], (H_A, 1, DV_A))
    return widen(up), widen(down), causal, cdec


def _swap_halves(w):
    half = w.shape[-1] // 2
    return jnp.concatenate([w[..., half:], w[..., :half]], axis=-1)


def kernel(x, a_w_in, a_w_out, b_w_in, b_q_norm, b_w_uq, b_w_out, kv_w_down, kv_norm, kv_w_up,
           ln_g, ln_b):
    batch, seq, _ = x.shape
    t = batch * seq
    x2 = x.reshape(t, D_MODEL)

    ang_a = _rope_angles(seq, DK_A // 2, ROPE_BASE_A)
    cos_a, sin_a = jnp.cos(ang_a), jnp.sin(ang_a)
    tab_b = _rope_lane_table(seq)

    qdec, kdec, causal, cdec = _retention_decays(RET_CHUNK, ROW_TILE)
    q, k, v, sg = _ret_inproj(x2, a_w_in[0].astype(BF16), cos_a, sin_a, qdec, kdec, seq)
    shape3 = lambda a: a.reshape(batch, seq, a.shape[-1])
    o = _retention(shape3(q), shape3(k), shape3(v), shape3(sg), causal, cdec)
    x1 = _outproj_ln(o.reshape(t, WIDTH_A), a_w_out[0].astype(BF16), x2,
                     ln_g[0][None, :], ln_b[0][None, :])

    w_lat = kv_w_down[:, :KV_LORA].astype(BF16)
    w_rope = kv_w_down[:, KV_LORA:]
    w_rope = jnp.concatenate([w_rope, _swap_halves(w_rope)], axis=-1).astype(BF16)
    qn, sgb, lat, kr = _mla_inproj(x1, b_w_in[0].astype(BF16), w_lat, w_rope,
                                   b_q_norm[0][None, :], kv_norm[None, :], tab_b, seq)

    wuq = b_w_uq[0].reshape(Q_LORA, H_B, QK_NOPE + QK_ROPE)
    wuq_n = wuq[..., :QK_NOPE].reshape(Q_LORA, H_B * QK_NOPE).astype(BF16)
    wuq_r = wuq[..., QK_NOPE:].reshape(Q_LORA, H_B * QK_ROPE).astype(BF16)
    qh = _q_up(qn, wuq_n, wuq_r, tab_b, batch, seq)

    wup = kv_w_up.reshape(KV_LORA, H_B, QK_NOPE + V_HEAD)
    wk = wup[..., :QK_NOPE].reshape(KV_LORA, H_B * QK_NOPE).astype(BF16)
    wvt = wup[..., QK_NOPE:].reshape(KV_LORA, H_B * V_HEAD).T.astype(BF16)
    kh, vt = _kv_up(lat, kr, wk, wvt, batch, seq)

    ob = _attention(qh, kh, vt, sgb.reshape(batch, seq, WIDTH_B))
    out = _outproj_ln(ob.reshape(t, WIDTH_B), b_w_out[0].astype(BF16), x1,
                      ln_g[1][None, :], ln_b[1][None, :])
    return out.reshape(batch, seq, D_MODEL)
```

```python
import functools
import math

import jax
import jax.numpy as jnp
from jax import lax
from jax.experimental import pallas as pl
from jax.experimental.pallas import tpu as pltpu

D_MODEL = 1024
DEPTH = 2

H_A = 4
DK_A = D_MODEL // H_A
DV_A = 2 * DK_A
WIDTH_A = H_A * DV_A
ROPE_BASE_A = 10000.0

H_B = 16
QK_NOPE = 128
QK_ROPE = 64
V_HEAD = 128
Q_LORA = 768
KV_LORA = 512
WIDTH_B = H_B * V_HEAD
ROPE_BASE_B = 10000.0
QK_PAD = 256
V_ROWS = V_HEAD + 16

DEEPNORM_ALPHA = (2.0 * DEPTH) ** 0.25

RET_CHUNK = 256
ROW_TILE = 512
WIDE_ROW_TILE = 1024
ROW_SLAB = 256
ATTN_TQ = 512
ATTN_TK = 512
MASK_VALUE = -1e30

VMEM_LIMIT_BYTES = 56 * 1024 * 1024

BF16 = jnp.bfloat16
F32 = jnp.float32


def _params(*semantics):
    return pltpu.CompilerParams(dimension_semantics=semantics,
                                vmem_limit_bytes=VMEM_LIMIT_BYTES)


def _resident(shape):
    zeros = (0,) * len(shape)
    return pl.BlockSpec(shape, lambda *_: zeros, pipeline_mode=pl.Buffered(1))


def _dot(a, b):
    return jnp.dot(a, b, preferred_element_type=F32)


def _dot_nt(a, b):
    return lax.dot_general(a, b, (((1,), (1,)), ((), ())), preferred_element_type=F32)


def _dot_tn(a, b):
    return lax.dot_general(a, b, (((0,), (0,)), ((), ())), preferred_element_type=F32)


def _row_slabs(rows):
    return [slice(i, i + ROW_SLAB) for i in range(0, rows, ROW_SLAB)]


def _silu(g):
    return g / (1.0 + jnp.exp(-g))


def _ret_inproj_kernel(x_ref, w_ref, cos_ref, sin_ref, qdec_ref, kdec_ref, q_ref, k_ref, v_ref, sg_ref):
    qk = H_A * DK_A
    half = DK_A // 2

    def rope_store(h, out_ref, rows, cos, sin, dec_ref):
        for i in range(H_A):
            lo = slice(i * DK_A, i * DK_A + half)
            hi = slice(i * DK_A + half, (i + 1) * DK_A)
            x1 = h[:, lo]
            x2 = h[:, hi]
            out_ref[rows, lo] = ((x1 * cos - x2 * sin) * dec_ref[rows, lo]).astype(BF16)
            out_ref[rows, hi] = ((x2 * cos + x1 * sin) * dec_ref[rows, hi]).astype(BF16)

    for rows in _row_slabs(x_ref.shape[0]):
        xb = x_ref[rows, :].astype(BF16)
        cos = cos_ref[rows, :]
        sin = sin_ref[rows, :]
        rope_store(_dot(xb, w_ref[:, 0:qk]), q_ref, rows, cos, sin, qdec_ref)
        rope_store(_dot(xb, w_ref[:, qk:2 * qk]), k_ref, rows, cos, sin, kdec_ref)
        v_ref[rows, :] = _dot(xb, w_ref[:, 2 * qk:2 * qk + WIDTH_A]).astype(BF16)
        sg_ref[rows, :] = _silu(_dot(xb, w_ref[:, 2 * qk + WIDTH_A:])).astype(BF16)


def _ret_inproj(x2, w, cos, sin, qdec, kdec, seq):
    t = x2.shape[0]
    tm = ROW_TILE
    nseq = seq // tm
    qk = H_A * DK_A
    row = lambda i: (i, 0)
    pos = lambda i: (i % nseq, 0)
    return pl.pallas_call(
        _ret_inproj_kernel,
        grid=(t // tm,),
        in_specs=[pl.BlockSpec((tm, D_MODEL), row),
                  _resident(w.shape),
                  pl.BlockSpec((tm, DK_A // 2), pos),
                  pl.BlockSpec((tm, DK_A // 2), pos),
                  _resident(qdec.shape), _resident(kdec.shape)],
        out_specs=[pl.BlockSpec((tm, qk), row), pl.BlockSpec((tm, qk), row),
                   pl.BlockSpec((tm, WIDTH_A), row), pl.BlockSpec((tm, WIDTH_A), row)],
        out_shape=[jax.ShapeDtypeStruct((t, qk), BF16), jax.ShapeDtypeStruct((t, qk), BF16),
                   jax.ShapeDtypeStruct((t, WIDTH_A), BF16), jax.ShapeDtypeStruct((t, WIDTH_A), BF16)],
        compiler_params=_params("arbitrary"),
        name="ret_inproj",
    )(x2, w, cos, sin, qdec, kdec)


def _retention_kernel(q_ref, k_ref, v_ref, sg_ref, causal_ref, cdec_ref, o_ref, state_ref, sbf_ref):
    @pl.when(pl.program_id(1) == 0)
    def _():
        state_ref[...] = jnp.zeros_like(state_ref)
        sbf_ref[...] = jnp.zeros_like(sbf_ref)

    for h in range(H_A):
        qk_cols = slice(h * DK_A, (h + 1) * DK_A)
        v_cols = slice(h * DV_A, (h + 1) * DV_A)
        q = q_ref[0, :, qk_cols]
        k = k_ref[0, :, qk_cols]
        v = v_ref[0, :, v_cols]
        scores = (_dot_nt(q, k) * causal_ref[...]).astype(BF16)
        o = _dot(scores, v) + _dot(q, sbf_ref[h])
        state = (state_ref[h] + _dot_tn(k, v)) * cdec_ref[h]
        state_ref[h] = state
        sbf_ref[h] = state.astype(BF16)

        mu = jnp.mean(o, axis=-1, keepdims=True)
        d = o - mu
        var = jnp.mean(d * d, axis=-1, keepdims=True)
        gated = d * lax.rsqrt(var + 1e-5) * sg_ref[0, :, v_cols].astype(F32)
        o_ref[0, :, v_cols] = gated.astype(BF16)


def _retention(q, k, v, sg, causal, cdec):
    b, s, _ = q.shape
    c = RET_CHUNK
    tok = lambda bi, ci: (bi, ci, 0)
    return pl.pallas_call(
        _retention_kernel,
        grid=(b, s // c),
        in_specs=[pl.BlockSpec((1, c, H_A * DK_A), tok), pl.BlockSpec((1, c, H_A * DK_A), tok),
                  pl.BlockSpec((1, c, WIDTH_A), tok), pl.BlockSpec((1, c, WIDTH_A), tok),
                  _resident(causal.shape), _resident(cdec.shape)],
        out_specs=pl.BlockSpec((1, c, WIDTH_A), tok),
        out_shape=jax.ShapeDtypeStruct((b, s, WIDTH_A), BF16),
        scratch_shapes=[pltpu.VMEM((H_A, DK_A, DV_A), F32), pltpu.VMEM((H_A, DK_A, DV_A), BF16)],
        compiler_params=_params("arbitrary", "arbitrary"),
        name="retention",
    )(q, k, v, sg, causal, cdec)


def _outproj_ln_kernel(a_ref, w_ref, x_ref, g_ref, b_ref, o_ref):
    for rows in _row_slabs(a_ref.shape[0]):
        z = DEEPNORM_ALPHA * x_ref[rows, :] + _dot(a_ref[rows, :], w_ref[...])
        mu = jnp.mean(z, axis=-1, keepdims=True)
        d = z - mu
        var = jnp.mean(d * d, axis=-1, keepdims=True)
        o_ref[rows, :] = d * lax.rsqrt(var + 1e-5) * g_ref[...] + b_ref[...]


def _outproj_ln(a, w, x2, g, bias):
    t, width = a.shape
    tm = WIDE_ROW_TILE
    row = lambda i: (i, 0)
    return pl.pallas_call(
        _outproj_ln_kernel,
        grid=(t // tm,),
        in_specs=[pl.BlockSpec((tm, width), row), _resident(w.shape),
                  pl.BlockSpec((tm, D_MODEL), row), _resident(g.shape), _resident(bias.shape)],
        out_specs=pl.BlockSpec((tm, D_MODEL), row),
        out_shape=jax.ShapeDtypeStruct((t, D_MODEL), F32),
        compiler_params=_params("arbitrary"),
        name="outproj_ln",
    )(a, w, x2, g, bias)


def _rope_pair(u, tab):
    lane = lax.broadcasted_iota(jnp.int32, u.shape, 1)
    t = u * jnp.where(lane < QK_ROPE, tab[:, 0:128], tab[:, 128:256])
    return jnp.where(lane < QK_ROPE, t + pltpu.roll(t, QK_ROPE, 1), 0.0)


def _rms(x, g, eps=1e-6):
    return x * lax.rsqrt(jnp.mean(x * x, axis=-1, keepdims=True) + eps) * g


def _mla_inproj_kernel(x_ref, win_ref, wlat_ref, wrope_ref, qg_ref, kvg_ref, tab_ref,
                       qn_ref, sg_ref, lat_ref, kr_ref):
    for rows in _row_slabs(x_ref.shape[0]):
        xb = x_ref[rows, :].astype(BF16)
        qn_ref[rows, :] = _rms(_dot(xb, win_ref[:, 0:Q_LORA]), qg_ref[...]).astype(BF16)
        sg_ref[rows, :] = _silu(_dot(xb, win_ref[:, Q_LORA:])).astype(BF16)
        lat_ref[rows, :] = _rms(_dot(xb, wlat_ref[...]), kvg_ref[...]).astype(BF16)
        kr_ref[rows, :] = _rope_pair(_dot(xb, wrope_ref[...]), tab_ref[rows, :]).astype(BF16)


def _mla_inproj(x2, win, wlat, wrope, qg, kvg, tab, seq):
    t = x2.shape[0]
    tm = WIDE_ROW_TILE
    nseq = seq // tm
    row = lambda i: (i, 0)
    pos = lambda i: (i % nseq, 0)
    widths = (Q_LORA, WIDTH_B, KV_LORA, 2 * QK_ROPE)
    return pl.pallas_call(
        _mla_inproj_kernel,
        grid=(t // tm,),
        in_specs=[pl.BlockSpec((tm, D_MODEL), row), _resident(win.shape), _resident(wlat.shape),
                  _resident(wrope.shape), _resident(qg.shape), _resident(kvg.shape),
                  pl.BlockSpec((tm, tab.shape[1]), pos)],
        out_specs=[pl.BlockSpec((tm, w), row) for w in widths],
        out_shape=[jax.ShapeDtypeStruct((t, w), BF16) for w in widths],
        compiler_params=_params("arbitrary"),
        name="mla_inproj",
    )(x2, win, wlat, wrope, qg, kvg, tab)


def _q_up_kernel(qn_ref, wnt_ref, wrt_ref, cos_ref, sin_ref, q_ref):
    scale = (QK_NOPE + QK_ROPE) ** -0.5 * math.log2(math.e)
    tq = ATTN_TQ
    half = QK_ROPE // 2
    zeros = jnp.zeros((QK_PAD - QK_NOPE - QK_ROPE, tq), BF16)
    for blk in range(qn_ref.shape[0] // tq):
        cols = slice(blk * tq, (blk + 1) * tq)
        qn = qn_ref[cols, :]
        cos = cos_ref[:, cols]
        sin = sin_ref[:, cols]
        yt_n = _dot_nt(wnt_ref[...], qn)
        yt_r = _dot_nt(wrt_ref[...], qn)
        for h in range(H_B):
            x1 = yt_r[h * QK_ROPE:h * QK_ROPE + half]
            x2 = yt_r[h * QK_ROPE + half:(h + 1) * QK_ROPE]
            q_ref[0, h, blk, 0:QK_NOPE, :] = (yt_n[h * QK_NOPE:(h + 1) * QK_NOPE] * scale).astype(BF16)
            q_ref[0, h, blk, QK_NOPE:QK_NOPE + half, :] = ((x1 * cos - x2 * sin) * scale).astype(BF16)
            q_ref[0, h, blk, QK_NOPE + half:QK_NOPE + QK_ROPE, :] = ((x2 * cos + x1 * sin) * scale).astype(BF16)
            q_ref[0, h, blk, QK_NOPE + QK_ROPE:, :] = zeros


def _q_up(qn, wnt, wrt, cos_t, sin_t, batch, seq):
    tm = WIDE_ROW_TILE
    tq = ATTN_TQ
    nseq = seq // tm
    pos = lambda b, i: (0, i)
    return pl.pallas_call(
        _q_up_kernel,
        grid=(batch, nseq),
        in_specs=[pl.BlockSpec((tm, Q_LORA), lambda b, i: (b * nseq + i, 0)),
                  _resident(wnt.shape), _resident(wrt.shape),
                  pl.BlockSpec((QK_ROPE // 2, tm), pos), pl.BlockSpec((QK_ROPE // 2, tm), pos)],
        out_specs=pl.BlockSpec((1, H_B, tm // tq, QK_PAD, tq), lambda b, i: (b, 0, i, 0, 0)),
        out_shape=jax.ShapeDtypeStruct((batch, H_B, seq // tq, QK_PAD, tq), BF16),
        compiler_params=_params("arbitrary", "arbitrary"),
        name="q_up",
    )(qn, wnt, wrt, cos_t, sin_t)


def _kv_up_kernel(lat_ref, kr_ref, wk_ref, wvt_ref, k_ref, vt_ref):
    tk = ATTN_TK
    for blk in range(lat_ref.shape[0] // tk):
        rows = slice(blk * tk, (blk + 1) * tk)
        lat = lat_ref[rows, :]
        kn = _dot(lat, wk_ref[...]).astype(BF16)
        kr = kr_ref[rows, :]
        for h in range(H_B):
            k_ref[0, h, rows, 0:QK_NOPE] = kn[:, h * QK_NOPE:(h + 1) * QK_NOPE]
            k_ref[0, h, rows, QK_NOPE:] = kr
        vt = _dot_nt(wvt_ref[...], lat).astype(BF16)
        vt_ref[0, :, blk, 0:V_HEAD, :] = vt.reshape(H_B, V_HEAD, tk)
        extra = lax.broadcasted_iota(jnp.int32, (H_B, V_ROWS - V_HEAD, tk), 1) == 0
        vt_ref[0, :, blk, V_HEAD:, :] = extra.astype(BF16)


def _kv_up(lat, kr, wk, wvt, batch, seq):
    tk = ATTN_TK
    tm = WIDE_ROW_TILE
    nk = seq // tk
    nstep = seq // tm
    row = lambda b, i: (b * nstep + i, 0)
    return pl.pallas_call(
        _kv_up_kernel,
        grid=(batch, nstep),
        in_specs=[pl.BlockSpec((tm, KV_LORA), row), pl.BlockSpec((tm, 2 * QK_ROPE), row),
                  _resident(wk.shape), _resident(wvt.shape)],
        out_specs=[pl.BlockSpec((1, H_B, tm, QK_PAD), lambda b, i: (b, 0, i, 0)),
                   pl.BlockSpec((1, H_B, tm // tk, V_ROWS, tk), lambda b, i: (b, 0, i, 0, 0))],
        out_shape=[jax.ShapeDtypeStruct((batch, H_B, seq, QK_PAD), BF16),
                   jax.ShapeDtypeStruct((batch, H_B, nk, V_ROWS, tk), BF16)],
        compiler_params=_params("arbitrary", "arbitrary"),
        name="kv_up",
    )(lat, kr, wk, wvt)


def _attn_kernel(q_ref, k_ref, vt_ref, sg_ref, o_ref, sa_ref, sb_ref, mxa_ref, mxb_ref, m_ref, acc_ref):
    tq = ATTN_TQ
    tk = ATTN_TK
    npairs = q_ref.shape[2] // 2
    last = npairs - 1

    def q_rows(pair, row):
        return q_ref[0, 0, 2 * pair + row]

    def scores(q, j, nblk, s_ref, mx_ref, off=0):
        kb = k_ref[0, 0, pl.ds(pl.multiple_of(j * tk, tk), nblk * tk), :]
        s = _dot(kb, q)
        s_ref[off:off + nblk * tk, :] = s
        mx_ref[...] = jnp.max(s, axis=0, keepdims=True)

    def update(row, j, nblk, s_ref, mx_ref, diag_at=None, off=0):
        s = s_ref[off:off + nblk * tk, :]
        if diag_at is None:
            block_max = mx_ref[...]
        else:
            kpos = lax.broadcasted_iota(jnp.int32, s.shape, 0)
            qpos = lax.broadcasted_iota(jnp.int32, s.shape, 1)
            s = jnp.where(kpos <= qpos + diag_at, s, MASK_VALUE)
            block_max = jnp.max(s, axis=0, keepdims=True)
        m = m_ref[row]
        m_new = jnp.maximum(m, block_max)
        alpha = jnp.exp2(m - m_new)
        p = jnp.exp2(s - m_new).astype(BF16)
        m_ref[row] = m_new
        pv = _dot(vt_ref[0, 0, j], p[0:tk])
        for i in range(1, nblk):
            pv += _dot(vt_ref[0, 0, j + i], p[i * tk:(i + 1) * tk])
        acc_ref[row] = alpha * acc_ref[row] + pv

    def finalize(pair, row):
        o = (acc_ref[row, 0:V_HEAD, :] / acc_ref[row, V_HEAD:V_HEAD + 1, :]).T
        rows = pl.ds(pl.multiple_of((2 * pair + row) * tq, tq), tq)
        o_ref[0, rows, :] = (o * sg_ref[0, rows, :].astype(F32)).astype(BF16)

    def row_pair(r, carry):
        m_ref[...] = jnp.full_like(m_ref, MASK_VALUE)
        acc_ref[...] = jnp.zeros_like(acc_ref)

        @pl.when(r == 0)
        def _():
            scores(q_rows(r, 0), 0, 1, sa_ref, mxa_ref, off=tk)

        @pl.when(r > 0)
        def _():
            def step(t):
                j = 2 * t
                scores(q_rows(r, 1), j, 2, sb_ref, mxb_ref)
                update(0, j, 2, sa_ref, mxa_ref)
                scores(q_rows(r, 0), jnp.minimum(j + 2, 2 * r - 1), 2, sa_ref, mxa_ref)
                update(1, j, 2, sb_ref, mxb_ref)

            def body(i, carry):
                step(2 * i)
                step(2 * i + 1)
                return carry

            lax.fori_loop(0, r // 2, body, 0)

            @pl.when(r % 2 == 1)
            def _():
                step(r - 1)

        scores(q_rows(r, 1), 2 * r, 2, sb_ref, mxb_ref)
        update(0, 2 * r, 1, sa_ref, mxa_ref, diag_at=0, off=tk)
        scores(q_rows(jnp.minimum(r + 1, last), 0), 0, 2, sa_ref, mxa_ref)
        update(1, 2 * r, 2, sb_ref, mxb_ref, diag_at=tk)
        finalize(r, 0)
        finalize(r, 1)
        return carry

    lax.fori_loop(0, npairs, row_pair, 0)


def _attention(q, k, vt, sg3):
    b, h, s, _ = k.shape
    tq = ATTN_TQ
    assert tq == ATTN_TK
    nk = s // ATTN_TK
    head = lambda bi, hi: (bi, hi, 0, 0)
    oblk = lambda bi, hi: (bi, 0, hi)
    return pl.pallas_call(
        _attn_kernel,
        grid=(b, h),
        in_specs=[pl.BlockSpec((1, 1, s // tq, QK_PAD, tq), lambda bi, hi: (bi, hi, 0, 0, 0)),
                  pl.BlockSpec((1, 1, s, QK_PAD), head),
                  pl.BlockSpec((1, 1, nk, V_ROWS, ATTN_TK), lambda bi, hi: (bi, hi, 0, 0, 0)),
                  pl.BlockSpec((1, s, V_HEAD), oblk)],
        out_specs=pl.BlockSpec((1, s, V_HEAD), oblk),
        out_shape=jax.ShapeDtypeStruct((b, s, WIDTH_B), BF16),
        scratch_shapes=[pltpu.VMEM((2 * ATTN_TK, tq), F32), pltpu.VMEM((2 * ATTN_TK, tq), F32),
                        pltpu.VMEM((1, tq), F32), pltpu.VMEM((1, tq), F32),
                        pltpu.VMEM((2, 1, tq), F32), pltpu.VMEM((2, V_ROWS, tq), F32)],
        compiler_params=_params("arbitrary", "arbitrary"),
        name="mla_attention",
    )(q, k, vt, sg3)


def _rope_angles(seq, half, base, copies=1):
    inv = jnp.tile(base ** (-jnp.arange(half, dtype=F32) / half), copies)
    return jnp.arange(seq, dtype=F32)[:, None] * inv[None, :]


def _rope_lane_table(seq):
    half = QK_ROPE // 2
    ang = _rope_angles(seq, half, ROPE_BASE_B, copies=128 // half)
    sign = jnp.where((jnp.arange(128) // half) % 2 == 0, -1.0, 1.0).astype(F32)
    return jnp.concatenate([jnp.cos(ang), jnp.sin(ang) * sign[None, :]], axis=-1)


def _retention_decays(c, rows):
    lg = jnp.log1p(-jnp.exp2(-5.0 - jnp.arange(H_A, dtype=F32)))
    idx = jnp.arange(c, dtype=F32)
    up = jnp.exp((idx + 1.0)[:, None] * lg[None, :])
    down = jnp.exp(-(idx + 1.0)[:, None] * lg[None, :]) * DK_A ** -0.5
    widen = lambda t: jnp.tile(jnp.repeat(t, DK_A, axis=1), (rows // c, 1))
    causal = (idx[:, None] >= idx[None, :]).astype(F32)
    cdec = jnp.broadcast_to(jnp.exp(c * lg)[:, None, None], (H_A, 1, DV_A))
    return widen(up), widen(down), causal, cdec


def _swap_halves(w):
    half = w.shape[-1] // 2
    return jnp.concatenate([w[..., half:], w[..., :half]], axis=-1)


def kernel(x, a_w_in, a_w_out, b_w_in, b_q_norm, b_w_uq, b_w_out, kv_w_down, kv_norm, kv_w_up,
           ln_g, ln_b):
    batch, seq, _ = x.shape
    t = batch * seq
    x2 = x.reshape(t, D_MODEL)

    ang_a = _rope_angles(seq, DK_A // 2, ROPE_BASE_A)
    cos_a, sin_a = jnp.cos(ang_a), jnp.sin(ang_a)
    tab_b = _rope_lane_table(seq)

    qdec, kdec, causal, cdec = _retention_decays(RET_CHUNK, ROW_TILE)
    q, k, v, sg = _ret_inproj(x2, a_w_in[0].astype(BF16), cos_a, sin_a, qdec, kdec, seq)
    shape3 = lambda a: a.reshape(batch, seq, a.shape[-1])
    o = _retention(shape3(q), shape3(k), shape3(v), shape3(sg), causal, cdec)
    x1 = _outproj_ln(o.reshape(t, WIDTH_A), a_w_out[0].astype(BF16), x2,
                     ln_g[0][None, :], ln_b[0][None, :])

    w_lat = kv_w_down[:, :KV_LORA].astype(BF16)
    w_rope = kv_w_down[:, KV_LORA:]
    w_rope = jnp.concatenate([w_rope, _swap_halves(w_rope)], axis=-1).astype(BF16)
    qn, sgb, lat, kr = _mla_inproj(x1, b_w_in[0].astype(BF16), w_lat, w_rope,
                                   b_q_norm[0][None, :], kv_norm[None, :], tab_b, seq)

    wuq_t = b_w_uq[0].T.reshape(H_B, QK_NOPE + QK_ROPE, Q_LORA)
    wuq_nt = wuq_t[:, :QK_NOPE].reshape(H_B * QK_NOPE, Q_LORA).astype(BF16)
    wuq_rt = wuq_t[:, QK_NOPE:].reshape(H_B * QK_ROPE, Q_LORA).astype(BF16)
    ang_t = _rope_angles(seq, QK_ROPE // 2, ROPE_BASE_B).T
    qh = _q_up(qn, wuq_nt, wuq_rt, jnp.cos(ang_t), jnp.sin(ang_t), batch, seq)

    wup = kv_w_up.reshape(KV_LORA, H_B, QK_NOPE + V_HEAD)
    wk = wup[..., :QK_NOPE].reshape(KV_LORA, H_B * QK_NOPE).astype(BF16)
    wvt = wup[..., QK_NOPE:].reshape(KV_LORA, H_B * V_HEAD).T.astype(BF16)
    kh, vt = _kv_up(lat, kr, wk, wvt, batch, seq)

    ob = _attention(qh, kh, vt, sgb.reshape(batch, seq, WIDTH_B))
    out = _outproj_ln(ob.reshape(t, WIDTH_B), b_w_out[0].astype(BF16), x1,
                      ln_g[1][None, :], ln_b[1][None, :])
    return out.reshape(batch, seq, D_MODEL)
```

```python
import functools
import math

import jax
import jax.numpy as jnp
from jax import lax
from jax.experimental import pallas as pl
from jax.experimental.pallas import tpu as pltpu

D_MODEL = 1024
DEPTH = 2

H_A = 4
DK_A = D_MODEL // H_A
DV_A = 2 * DK_A
WIDTH_A = H_A * DV_A
ROPE_BASE_A = 10000.0

H_B = 16
QK_NOPE = 128
QK_ROPE = 64
V_HEAD = 128
Q_LORA = 768
KV_LORA = 512
WIDTH_B = H_B * V_HEAD
ROPE_BASE_B = 10000.0
QK_PAD = 256
V_ROWS = V_HEAD + 16

DEEPNORM_ALPHA = (2.0 * DEPTH) ** 0.25

RET_CHUNK = 256
ROW_TILE = 512
WIDE_ROW_TILE = 1024
ROW_SLAB = 256
ATTN_TQ = 512
ATTN_TK = 512
MASK_VALUE = -1e30

VMEM_LIMIT_BYTES = 56 * 1024 * 1024

BF16 = jnp.bfloat16
F32 = jnp.float32


def _params(*semantics):
    return pltpu.CompilerParams(dimension_semantics=semantics,
                                vmem_limit_bytes=VMEM_LIMIT_BYTES)


def _resident(shape):
    zeros = (0,) * len(shape)
    return pl.BlockSpec(shape, lambda *_: zeros, pipeline_mode=pl.Buffered(1))


def _dot(a, b):
    return jnp.dot(a, b, preferred_element_type=F32)


def _dot_nt(a, b):
    return lax.dot_general(a, b, (((1,), (1,)), ((), ())), preferred_element_type=F32)


def _dot_tn(a, b):
    return lax.dot_general(a, b, (((0,), (0,)), ((), ())), preferred_element_type=F32)


def _row_slabs(rows):
    return [slice(i, i + ROW_SLAB) for i in range(0, rows, ROW_SLAB)]


def _silu(g):
    return g / (1.0 + jnp.exp(-g))


def _ret_inproj_kernel(x_ref, w_ref, cos_ref, sin_ref, qdec_ref, kdec_ref, q_ref, k_ref, v_ref, sg_ref):
    qk = H_A * DK_A
    half = DK_A // 2

    def rope_store(h, out_ref, rows, cos, sin, dec_ref):
        for i in range(H_A):
            lo = slice(i * DK_A, i * DK_A + half)
            hi = slice(i * DK_A + half, (i + 1) * DK_A)
            x1 = h[:, lo]
            x2 = h[:, hi]
            out_ref[rows, lo] = ((x1 * cos - x2 * sin) * dec_ref[rows, lo]).astype(BF16)
            out_ref[rows, hi] = ((x2 * cos + x1 * sin) * dec_ref[rows, hi]).astype(BF16)

    for rows in _row_slabs(x_ref.shape[0]):
        xb = x_ref[rows, :].astype(BF16)
        cos = cos_ref[rows, :]
        sin = sin_ref[rows, :]
        rope_store(_dot(xb, w_ref[:, 0:qk]), q_ref, rows, cos, sin, qdec_ref)
        rope_store(_dot(xb, w_ref[:, qk:2 * qk]), k_ref, rows, cos, sin, kdec_ref)
        v_ref[rows, :] = _dot(xb, w_ref[:, 2 * qk:2 * qk + WIDTH_A]).astype(BF16)
        sg_ref[rows, :] = _silu(_dot(xb, w_ref[:, 2 * qk + WIDTH_A:])).astype(BF16)


def _ret_inproj(x2, w, cos, sin, qdec, kdec, seq):
    t = x2.shape[0]
    tm = ROW_TILE
    nseq = seq // tm
    qk = H_A * DK_A
    row = lambda i: (i, 0)
    pos = lambda i: (i % nseq, 0)
    return pl.pallas_call(
        _ret_inproj_kernel,
        grid=(t // tm,),
        in_specs=[pl.BlockSpec((tm, D_MODEL), row),
                  _resident(w.shape),
                  pl.BlockSpec((tm, DK_A // 2), pos),
                  pl.BlockSpec((tm, DK_A // 2), pos),
                  _resident(qdec.shape), _resident(kdec.shape)],
        out_specs=[pl.BlockSpec((tm, qk), row), pl.BlockSpec((tm, qk), row),
                   pl.BlockSpec((tm, WIDTH_A), row), pl.BlockSpec((tm, WIDTH_A), row)],
        out_shape=[jax.ShapeDtypeStruct((t, qk), BF16), jax.ShapeDtypeStruct((t, qk), BF16),
                   jax.ShapeDtypeStruct((t, WIDTH_A), BF16), jax.ShapeDtypeStruct((t, WIDTH_A), BF16)],
        compiler_params=_params("arbitrary"),
        name="ret_inproj",
    )(x2, w, cos, sin, qdec, kdec)


def _retention_kernel(q_ref, k_ref, v_ref, sg_ref, causal_ref, cdec_ref, o_ref, state_ref, sbf_ref):
    @pl.when(pl.program_id(1) == 0)
    def _():
        state_ref[...] = jnp.zeros_like(state_ref)
        sbf_ref[...] = jnp.zeros_like(sbf_ref)

    for h in range(H_A):
        qk_cols = slice(h * DK_A, (h + 1) * DK_A)
        v_cols = slice(h * DV_A, (h + 1) * DV_A)
        q = q_ref[0, :, qk_cols]
        k = k_ref[0, :, qk_cols]
        v = v_ref[0, :, v_cols]
        scores = (_dot_nt(q, k) * causal_ref[...]).astype(BF16)
        o = _dot(scores, v) + _dot(q, sbf_ref[h])
        state = (state_ref[h] + _dot_tn(k, v)) * cdec_ref[h]
        state_ref[h] = state
        sbf_ref[h] = state.astype(BF16)

        mu = jnp.mean(o, axis=-1, keepdims=True)
        d = o - mu
        var = jnp.mean(d * d, axis=-1, keepdims=True)
        gated = d * lax.rsqrt(var + 1e-5) * sg_ref[0, :, v_cols].astype(F32)
        o_ref[0, :, v_cols] = gated.astype(BF16)


def _retention(q, k, v, sg, causal, cdec):
    b, s, _ = q.shape
    c = RET_CHUNK
    tok = lambda bi, ci: (bi, ci, 0)
    return pl.pallas_call(
        _retention_kernel,
        grid=(b, s // c),
        in_specs=[pl.BlockSpec((1, c, H_A * DK_A), tok), pl.BlockSpec((1, c, H_A * DK_A), tok),
                  pl.BlockSpec((1, c, WIDTH_A), tok), pl.BlockSpec((1, c, WIDTH_A), tok),
                  _resident(causal.shape), _resident(cdec.shape)],
        out_specs=pl.BlockSpec((1, c, WIDTH_A), tok),
        out_shape=jax.ShapeDtypeStruct((b, s, WIDTH_A), BF16),
        scratch_shapes=[pltpu.VMEM((H_A, DK_A, DV_A), F32), pltpu.VMEM((H_A, DK_A, DV_A), BF16)],
        compiler_params=_params("arbitrary", "arbitrary"),
        name="retention",
    )(q, k, v, sg, causal, cdec)


def _outproj_ln_kernel(a_ref, w_ref, x_ref, g_ref, b_ref, o_ref):
    for rows in _row_slabs(a_ref.shape[0]):
        z = DEEPNORM_ALPHA * x_ref[rows, :] + _dot(a_ref[rows, :], w_ref[...])
        mu = jnp.mean(z, axis=-1, keepdims=True)
        d = z - mu
        var = jnp.mean(d * d, axis=-1, keepdims=True)
        o_ref[rows, :] = d * lax.rsqrt(var + 1e-5) * g_ref[...] + b_ref[...]


def _outproj_ln(a, w, x2, g, bias):
    t, width = a.shape
    tm = WIDE_ROW_TILE
    row = lambda i: (i, 0)
    return pl.pallas_call(
        _outproj_ln_kernel,
        grid=(t // tm,),
        in_specs=[pl.BlockSpec((tm, width), row), _resident(w.shape),
                  pl.BlockSpec((tm, D_MODEL), row), _resident(g.shape), _resident(bias.shape)],
        out_specs=pl.BlockSpec((tm, D_MODEL), row),
        out_shape=jax.ShapeDtypeStruct((t, D_MODEL), F32),
        compiler_params=_params("arbitrary"),
        name="outproj_ln",
    )(a, w, x2, g, bias)


def _rope_pair(u, tab):
    lane = lax.broadcasted_iota(jnp.int32, u.shape, 1)
    t = u * jnp.where(lane < QK_ROPE, tab[:, 0:128], tab[:, 128:256])
    return jnp.where(lane < QK_ROPE, t + pltpu.roll(t, QK_ROPE, 1), 0.0)


def _rms(x, g, eps=1e-6):
    return x * lax.rsqrt(jnp.mean(x * x, axis=-1, keepdims=True) + eps) * g


def _mla_inproj_kernel(x_ref, win_ref, wdn_ref, qg_ref, kvg_ref, tab_ref,
                       qn_ref, sg_ref, lat_ref, kr_ref):
    for rows in _row_slabs(x_ref.shape[0]):
        xb = x_ref[rows, :].astype(BF16)
        qn_ref[rows, :] = _rms(_dot(xb, win_ref[:, 0:Q_LORA]), qg_ref[...]).astype(BF16)
        sg_ref[rows, :] = _silu(_dot(xb, win_ref[:, Q_LORA:])).astype(BF16)
        c = _dot(xb, wdn_ref[...])
        lat_ref[rows, :] = _rms(c[:, 0:KV_LORA], kvg_ref[...]).astype(BF16)
        kr_ref[rows, :] = _rope_pair(c[:, KV_LORA:], tab_ref[rows, :]).astype(BF16)


def _mla_inproj(x2, win, wdn, qg, kvg, tab, seq):
    t = x2.shape[0]
    tm = WIDE_ROW_TILE
    nseq = seq // tm
    row = lambda i: (i, 0)
    pos = lambda i: (i % nseq, 0)
    widths = (Q_LORA, WIDTH_B, KV_LORA, 2 * QK_ROPE)
    return pl.pallas_call(
        _mla_inproj_kernel,
        grid=(t // tm,),
        in_specs=[pl.BlockSpec((tm, D_MODEL), row), _resident(win.shape), _resident(wdn.shape),
                  _resident(qg.shape), _resident(kvg.shape),
                  pl.BlockSpec((tm, tab.shape[1]), pos)],
        out_specs=[pl.BlockSpec((tm, w), row) for w in widths],
        out_shape=[jax.ShapeDtypeStruct((t, w), BF16) for w in widths],
        compiler_params=_params("arbitrary"),
        name="mla_inproj",
    )(x2, win, wdn, qg, kvg, tab)


def _q_up_kernel(qn_ref, wnt_ref, wrt_ref, cos_ref, sin_ref, q_ref):
    scale = (QK_NOPE + QK_ROPE) ** -0.5 * math.log2(math.e)
    tq = ATTN_TQ
    half = QK_ROPE // 2
    zeros = jnp.zeros((QK_PAD - QK_NOPE - QK_ROPE, tq), BF16)
    for blk in range(qn_ref.shape[0] // tq):
        cols = slice(blk * tq, (blk + 1) * tq)
        qn = qn_ref[cols, :]
        cos = cos_ref[:, cols]
        sin = sin_ref[:, cols]
        yt_n = _dot_nt(wnt_ref[...], qn)
        yt_r = _dot_nt(wrt_ref[...], qn)
        for h in range(H_B):
            x1 = yt_r[h * QK_ROPE:h * QK_ROPE + half]
            x2 = yt_r[h * QK_ROPE + half:(h + 1) * QK_ROPE]
            q_ref[0, h, blk, 0:QK_NOPE, :] = (yt_n[h * QK_NOPE:(h + 1) * QK_NOPE] * scale).astype(BF16)
            q_ref[0, h, blk, QK_NOPE:QK_NOPE + half, :] = ((x1 * cos - x2 * sin) * scale).astype(BF16)
            q_ref[0, h, blk, QK_NOPE + half:QK_NOPE + QK_ROPE, :] = ((x2 * cos + x1 * sin) * scale).astype(BF16)
            q_ref[0, h, blk, QK_NOPE + QK_ROPE:, :] = zeros


def _q_up(qn, wnt, wrt, cos_t, sin_t, batch, seq):
    tm = WIDE_ROW_TILE
    tq = ATTN_TQ
    nseq = seq // tm
    pos = lambda b, i: (0, i)
    return pl.pallas_call(
        _q_up_kernel,
        grid=(batch, nseq),
        in_specs=[pl.BlockSpec((tm, Q_LORA), lambda b, i: (b * nseq + i, 0)),
                  _resident(wnt.shape), _resident(wrt.shape),
                  pl.BlockSpec((QK_ROPE // 2, tm), pos), pl.BlockSpec((QK_ROPE // 2, tm), pos)],
        out_specs=pl.BlockSpec((1, H_B, tm // tq, QK_PAD, tq), lambda b, i: (b, 0, i, 0, 0)),
        out_shape=jax.ShapeDtypeStruct((batch, H_B, seq // tq, QK_PAD, tq), BF16),
        compiler_params=_params("arbitrary", "arbitrary"),
        name="q_up",
    )(qn, wnt, wrt, cos_t, sin_t)


def _kv_up_kernel(lat_ref, kr_ref, wk_ref, wvt_ref, k_ref, vt_ref):
    tk = ATTN_TK
    for blk in range(lat_ref.shape[0] // tk):
        rows = slice(blk * tk, (blk + 1) * tk)
        lat = lat_ref[rows, :]
        kn = _dot(lat, wk_ref[...]).astype(BF16)
        kr = kr_ref[rows, :]
        for h in range(H_B):
            k_ref[0, h, rows, 0:QK_NOPE] = kn[:, h * QK_NOPE:(h + 1) * QK_NOPE]
            k_ref[0, h, rows, QK_NOPE:] = kr
        vt = _dot_nt(wvt_ref[...], lat).astype(BF16)
        vt_ref[0, :, blk, 0:V_HEAD, :] = vt.reshape(H_B, V_HEAD, tk)
        extra = lax.broadcasted_iota(jnp.int32, (H_B, V_ROWS - V_HEAD, tk), 1) == 0
        vt_ref[0, :, blk, V_HEAD:, :] = extra.astype(BF16)


def _kv_up(lat, kr, wk, wvt, batch, seq):
    tk = ATTN_TK
    tm = WIDE_ROW_TILE
    nk = seq // tk
    nstep = seq // tm
    row = lambda b, i: (b * nstep + i, 0)
    return pl.pallas_call(
        _kv_up_kernel,
        grid=(batch, nstep),
        in_specs=[pl.BlockSpec((tm, KV_LORA), row), pl.BlockSpec((tm, 2 * QK_ROPE), row),
                  _resident(wk.shape), _resident(wvt.shape)],
        out_specs=[pl.BlockSpec((1, H_B, tm, QK_PAD), lambda b, i: (b, 0, i, 0)),
                   pl.BlockSpec((1, H_B, tm // tk, V_ROWS, tk), lambda b, i: (b, 0, i, 0, 0))],
        out_shape=[jax.ShapeDtypeStruct((batch, H_B, seq, QK_PAD), BF16),
                   jax.ShapeDtypeStruct((batch, H_B, nk, V_ROWS, tk), BF16)],
        compiler_params=_params("arbitrary", "arbitrary"),
        name="kv_up",
    )(lat, kr, wk, wvt)


def _attn_kernel(q_ref, k_ref, vt_ref, sg_ref, o_ref, sa_ref, sb_ref, mxa_ref, mxb_ref, m_ref, acc_ref,
                 hold_ref):
    tq = ATTN_TQ
    tk = ATTN_TK
    npairs = q_ref.shape[2] // 2
    last = npairs - 1

    def q_rows(pair, row):
        return q_ref[0, 0, 2 * pair + row]

    def scores(q, j, nblk, s_ref, mx_ref, off=0):
        kb = k_ref[0, 0, pl.ds(pl.multiple_of(j * tk, tk), nblk * tk), :]
        s = _dot(kb, q)
        s_ref[off:off + nblk * tk, :] = s
        mx_ref[...] = jnp.max(s, axis=0, keepdims=True)

    def update(row, j, nblk, s_ref, mx_ref, diag_at=None, off=0):
        s = s_ref[off:off + nblk * tk, :]
        if diag_at is None:
            block_max = mx_ref[...]
        else:
            kpos = lax.broadcasted_iota(jnp.int32, s.shape, 0)
            qpos = lax.broadcasted_iota(jnp.int32, s.shape, 1)
            s = jnp.where(kpos <= qpos + diag_at, s, MASK_VALUE)
            block_max = jnp.max(s, axis=0, keepdims=True)
        m = m_ref[row]
        m_new = jnp.maximum(m, block_max)
        alpha = jnp.exp2(m - m_new)
        p = jnp.exp2(s - m_new).astype(BF16)
        m_ref[row] = m_new
        pv = _dot(vt_ref[0, 0, j], p[0:tk])
        for i in range(1, nblk):
            pv += _dot(vt_ref[0, 0, j + i], p[i * tk:(i + 1) * tk])
        acc_ref[row] = alpha * acc_ref[row] + pv

    def finalize(pair, row, acc):
        o = (acc[0:V_HEAD, :] / acc[V_HEAD:V_HEAD + 1, :]).T
        rows = pl.ds(pl.multiple_of((2 * pair + row) * tq, tq), tq)
        o_ref[0, rows, :] = (o * sg_ref[0, rows, :].astype(F32)).astype(BF16)

    def row_pair(r, carry):
        m_ref[...] = jnp.full_like(m_ref, MASK_VALUE)
        acc_ref[...] = jnp.zeros_like(acc_ref)

        @pl.when(r == 0)
        def _():
            scores(q_rows(r, 0), 0, 1, sa_ref, mxa_ref, off=tk)
            hold_ref[...] = jnp.ones_like(hold_ref)

        @pl.when(r > 0)
        def _():
            def step(t):
                j = 2 * t
                scores(q_rows(r, 1), j, 2, sb_ref, mxb_ref)
                update(0, j, 2, sa_ref, mxa_ref)
                scores(q_rows(r, 0), jnp.minimum(j + 2, 2 * r - 1), 2, sa_ref, mxa_ref)
                update(1, j, 2, sb_ref, mxb_ref)

            def body(i, carry):
                step(2 * i)
                step(2 * i + 1)
                return carry

            lax.fori_loop(0, r // 2, body, 0)

            @pl.when(r % 2 == 1)
            def _():
                step(r - 1)

        finalize(jnp.maximum(r - 1, 0), 1, hold_ref[...])
        scores(q_rows(r, 1), 2 * r, 2, sb_ref, mxb_ref)
        update(0, 2 * r, 1, sa_ref, mxa_ref, diag_at=0, off=tk)
        scores(q_rows(jnp.minimum(r + 1, last), 0), 0, 2, sa_ref, mxa_ref)
        update(1, 2 * r, 2, sb_ref, mxb_ref, diag_at=tk)
        finalize(r, 0, acc_ref[0])
        hold_ref[...] = acc_ref[1]
        return carry

    lax.fori_loop(0, npairs, row_pair, 0)
    finalize(last, 1, hold_ref[...])


def _attention(q, k, vt, sg3):
    b, h, s, _ = k.shape
    tq = ATTN_TQ
    assert tq == ATTN_TK
    nk = s // ATTN_TK
    head = lambda bi, hi: (bi, hi, 0, 0)
    oblk = lambda bi, hi: (bi, 0, hi)
    return pl.pallas_call(
        _attn_kernel,
        grid=(b, h),
        in_specs=[pl.BlockSpec((1, 1, s // tq, QK_PAD, tq), lambda bi, hi: (bi, hi, 0, 0, 0)),
                  pl.BlockSpec((1, 1, s, QK_PAD), head),
                  pl.BlockSpec((1, 1, nk, V_ROWS, ATTN_TK), lambda bi, hi: (bi, hi, 0, 0, 0)),
                  pl.BlockSpec((1, s, V_HEAD), oblk)],
        out_specs=pl.BlockSpec((1, s, V_HEAD), oblk),
        out_shape=jax.ShapeDtypeStruct((b, s, WIDTH_B), BF16),
        scratch_shapes=[pltpu.VMEM((2 * ATTN_TK, tq), F32), pltpu.VMEM((2 * ATTN_TK, tq), F32),
                        pltpu.VMEM((1, tq), F32), pltpu.VMEM((1, tq), F32),
                        pltpu.VMEM((2, 1, tq), F32), pltpu.VMEM((2, V_ROWS, tq), F32),
                        pltpu.VMEM((V_ROWS, tq), F32)],
        compiler_params=_params("arbitrary", "arbitrary"),
        name="mla_attention",
    )(q, k, vt, sg3)


def _rope_angles(seq, half, base, copies=1):
    inv = jnp.tile(base ** (-jnp.arange(half, dtype=F32) / half), copies)
    return jnp.arange(seq, dtype=F32)[:, None] * inv[None, :]


def _rope_lane_table(seq):
    half = QK_ROPE // 2
    ang = _rope_angles(seq, half, ROPE_BASE_B, copies=128 // half)
    sign = jnp.where((jnp.arange(128) // half) % 2 == 0, -1.0, 1.0).astype(F32)
    return jnp.concatenate([jnp.cos(ang), jnp.sin(ang) * sign[None, :]], axis=-1)


def _retention_decays(c, rows):
    lg = jnp.log1p(-jnp.exp2(-5.0 - jnp.arange(H_A, dtype=F32)))
    idx = jnp.arange(c, dtype=F32)
    up = jnp.exp((idx + 1.0)[:, None] * lg[None, :])
    down = jnp.exp(-(idx + 1.0)[:, None] * lg[None, :]) * DK_A ** -0.5
    widen = lambda t: jnp.tile(jnp.repeat(t, DK_A, axis=1), (rows // c, 1))
    causal = (idx[:, None] >= idx[None, :]).astype(F32)
    cdec = jnp.broadcast_to(jnp.exp(c * lg)[:, None, None], (H_A, 1, DV_A))
    return widen(up), widen(down), causal, cdec


def _swap_halves(w):
    half = w.shape[-1] // 2
    return jnp.concatenate([w[..., half:], w[..., :half]], axis=-1)


def kernel(x, a_w_in, a_w_out, b_w_in, b_q_norm, b_w_uq, b_w_out, kv_w_down, kv_norm, kv_w_up,
           ln_g, ln_b):
    batch, seq, _ = x.shape
    t = batch * seq
    x2 = x.reshape(t, D_MODEL)

    ang_a = _rope_angles(seq, DK_A // 2, ROPE_BASE_A)
    cos_a, sin_a = jnp.cos(ang_a), jnp.sin(ang_a)
    tab_b = _rope_lane_table(seq)

    qdec, kdec, causal, cdec = _retention_decays(RET_CHUNK, ROW_TILE)
    q, k, v, sg = _ret_inproj(x2, a_w_in[0].astype(BF16), cos_a, sin_a, qdec, kdec, seq)
    shape3 = lambda a: a.reshape(batch, seq, a.shape[-1])
    o = _retention(shape3(q), shape3(k), shape3(v), shape3(sg), causal, cdec)
    x1 = _outproj_ln(o.reshape(t, WIDTH_A), a_w_out[0].astype(BF16), x2,
                     ln_g[0][None, :], ln_b[0][None, :])

    w_lat = kv_w_down[:, :KV_LORA].astype(BF16)
    w_rope = kv_w_down[:, KV_LORA:]
    w_rope = jnp.concatenate([w_rope, _swap_halves(w_rope)], axis=-1).astype(BF16)
    wdn = jnp.concatenate([w_lat, w_rope], axis=-1)
    qn, sgb, lat, kr = _mla_inproj(x1, b_w_in[0].astype(BF16), wdn,
                                   b_q_norm[0][None, :], kv_norm[None, :], tab_b, seq)

    wuq_t = b_w_uq[0].T.reshape(H_B, QK_NOPE + QK_ROPE, Q_LORA)
    wuq_nt = wuq_t[:, :QK_NOPE].reshape(H_B * QK_NOPE, Q_LORA).astype(BF16)
    wuq_rt = wuq_t[:, QK_NOPE:].reshape(H_B * QK_ROPE, Q_LORA).astype(BF16)
    ang_t = _rope_angles(seq, QK_ROPE // 2, ROPE_BASE_B).T
    qh = _q_up(qn, wuq_nt, wuq_rt, jnp.cos(ang_t), jnp.sin(ang_t), batch, seq)

    wup = kv_w_up.reshape(KV_LORA, H_B, QK_NOPE + V_HEAD)
    wk = wup[..., :QK_NOPE].reshape(KV_LORA, H_B * QK_NOPE).astype(BF16)
    wvt = wup[..., QK_NOPE:].reshape(KV_LORA, H_B * V_HEAD).T.astype(BF16)
    kh, vt = _kv_up(lat, kr, wk, wvt, batch, seq)

    ob = _attention(qh, kh, vt, sgb.reshape(batch, seq, WIDTH_B))
    out = _outproj_ln(ob.reshape(t, WIDTH_B), b_w_out[0].astype(BF16), x1,
                      ln_g[1][None, :], ln_b[1][None, :])
    return out.reshape(batch, seq, D_MODEL)
```

```python
import math

import jax
import jax.numpy as jnp
from jax import lax
from jax.experimental import pallas as pl
from jax.experimental.pallas import tpu as pltpu

D_MODEL = 1024
DEPTH = 2

H_A = 4
DK_A = D_MODEL // H_A
DV_A = 2 * DK_A
WIDTH_A = H_A * DV_A
ROPE_BASE_A = 10000.0

H_B = 16
QK_NOPE = 128
QK_ROPE = 64
V_HEAD = 128
Q_LORA = 768
KV_LORA = 512
WIDTH_B = H_B * V_HEAD
ROPE_BASE_B = 10000.0
QK_PAD = 256
BF16_SUBLANE_TILE = 16
V_ROWS = V_HEAD + BF16_SUBLANE_TILE

DEEPNORM_ALPHA = (2.0 * DEPTH) ** 0.25

RET_CHUNK = 256
ROW_TILE = 512
WIDE_ROW_TILE = 1024
ROW_SLAB = 256
ATTN_TQ = 512
ATTN_TK = 512
MASK_VALUE = -1e30

VMEM_LIMIT_BYTES = 56 * 1024 * 1024

BF16 = jnp.bfloat16
F32 = jnp.float32


def _params(*semantics):
    return pltpu.CompilerParams(dimension_semantics=semantics,
                                vmem_limit_bytes=VMEM_LIMIT_BYTES)


def _resident(shape):
    zeros = (0,) * len(shape)
    return pl.BlockSpec(shape, lambda *_: zeros, pipeline_mode=pl.Buffered(1))


def _dot(a, b):
    return jnp.dot(a, b, preferred_element_type=F32)


def _dot_nt(a, b):
    return lax.dot_general(a, b, (((1,), (1,)), ((), ())), preferred_element_type=F32)


def _dot_tn(a, b):
    return lax.dot_general(a, b, (((0,), (0,)), ((), ())), preferred_element_type=F32)


def _row_slabs(rows):
    return [slice(i, i + ROW_SLAB) for i in range(0, rows, ROW_SLAB)]


def _silu(g):
    return g / (1.0 + jnp.exp(-g))


def _ret_inproj_kernel(x_ref, w_ref, cos_ref, sin_ref, qdec_ref, kdec_ref, q_ref, k_ref, v_ref, sg_ref):
    qk = H_A * DK_A
    half = DK_A // 2

    def rope_store(h, out_ref, rows, cos, sin, dec_ref):
        for i in range(H_A):
            lo = slice(i * DK_A, i * DK_A + half)
            hi = slice(i * DK_A + half, (i + 1) * DK_A)
            x1 = h[:, lo]
            x2 = h[:, hi]
            out_ref[rows, lo] = ((x1 * cos - x2 * sin) * dec_ref[rows, lo]).astype(BF16)
            out_ref[rows, hi] = ((x2 * cos + x1 * sin) * dec_ref[rows, hi]).astype(BF16)

    for rows in _row_slabs(x_ref.shape[0]):
        xb = x_ref[rows, :].astype(BF16)
        cos = cos_ref[rows, :]
        sin = sin_ref[rows, :]
        rope_store(_dot(xb, w_ref[:, 0:qk]), q_ref, rows, cos, sin, qdec_ref)
        rope_store(_dot(xb, w_ref[:, qk:2 * qk]), k_ref, rows, cos, sin, kdec_ref)
        v_ref[rows, :] = _dot(xb, w_ref[:, 2 * qk:2 * qk + WIDTH_A]).astype(BF16)
        sg_ref[rows, :] = _silu(_dot(xb, w_ref[:, 2 * qk + WIDTH_A:])).astype(BF16)


def _ret_inproj(x2, w, cos, sin, qdec, kdec, seq):
    t = x2.shape[0]
    tm = ROW_TILE
    nseq = seq // tm
    qk = H_A * DK_A
    row = lambda i: (i, 0)
    pos = lambda i: (i % nseq, 0)
    return pl.pallas_call(
        _ret_inproj_kernel,
        grid=(t // tm,),
        in_specs=[pl.BlockSpec((tm, D_MODEL), row),
                  _resident(w.shape),
                  pl.BlockSpec((tm, DK_A // 2), pos),
                  pl.BlockSpec((tm, DK_A // 2), pos),
                  _resident(qdec.shape), _resident(kdec.shape)],
        out_specs=[pl.BlockSpec((tm, qk), row), pl.BlockSpec((tm, qk), row),
                   pl.BlockSpec((tm, WIDTH_A), row), pl.BlockSpec((tm, WIDTH_A), row)],
        out_shape=[jax.ShapeDtypeStruct((t, qk), BF16), jax.ShapeDtypeStruct((t, qk), BF16),
                   jax.ShapeDtypeStruct((t, WIDTH_A), BF16), jax.ShapeDtypeStruct((t, WIDTH_A), BF16)],
        compiler_params=_params("arbitrary"),
        name="ret_inproj",
    )(x2, w, cos, sin, qdec, kdec)


def _retention_kernel(q_ref, k_ref, v_ref, sg_ref, causal_ref, cdec_ref, o_ref, state_ref, sbf_ref):
    @pl.when(pl.program_id(1) == 0)
    def _():
        state_ref[...] = jnp.zeros_like(state_ref)
        sbf_ref[...] = jnp.zeros_like(sbf_ref)

    for h in range(H_A):
        qk_cols = slice(h * DK_A, (h + 1) * DK_A)
        v_cols = slice(h * DV_A, (h + 1) * DV_A)
        q = q_ref[0, :, qk_cols]
        k = k_ref[0, :, qk_cols]
        v = v_ref[0, :, v_cols]
        scores = (_dot_nt(q, k) * causal_ref[...]).astype(BF16)
        o = _dot(scores, v) + _dot(q, sbf_ref[h])
        state = (state_ref[h] + _dot_tn(k, v)) * cdec_ref[h]
        state_ref[h] = state
        sbf_ref[h] = state.astype(BF16)

        mu = jnp.mean(o, axis=-1, keepdims=True)
        d = o - mu
        var = jnp.mean(d * d, axis=-1, keepdims=True)
        gated = d * lax.rsqrt(var + 1e-5) * sg_ref[0, :, v_cols].astype(F32)
        o_ref[0, :, v_cols] = gated.astype(BF16)


def _retention(q, k, v, sg, causal, cdec):
    b, s, _ = q.shape
    c = RET_CHUNK
    tok = lambda bi, ci: (bi, ci, 0)
    return pl.pallas_call(
        _retention_kernel,
        grid=(b, s // c),
        in_specs=[pl.BlockSpec((1, c, H_A * DK_A), tok), pl.BlockSpec((1, c, H_A * DK_A), tok),
                  pl.BlockSpec((1, c, WIDTH_A), tok), pl.BlockSpec((1, c, WIDTH_A), tok),
                  _resident(causal.shape), _resident(cdec.shape)],
        out_specs=pl.BlockSpec((1, c, WIDTH_A), tok),
        out_shape=jax.ShapeDtypeStruct((b, s, WIDTH_A), BF16),
        scratch_shapes=[pltpu.VMEM((H_A, DK_A, DV_A), F32), pltpu.VMEM((H_A, DK_A, DV_A), BF16)],
        compiler_params=_params("arbitrary", "arbitrary"),
        name="retention",
    )(q, k, v, sg, causal, cdec)


def _outproj_ln_kernel(a_ref, w_ref, x_ref, g_ref, b_ref, o_ref):
    for rows in _row_slabs(a_ref.shape[0]):
        z = DEEPNORM_ALPHA * x_ref[rows, :] + _dot(a_ref[rows, :], w_ref[...])
        mu = jnp.mean(z, axis=-1, keepdims=True)
        d = z - mu
        var = jnp.mean(d * d, axis=-1, keepdims=True)
        o_ref[rows, :] = d * lax.rsqrt(var + 1e-5) * g_ref[...] + b_ref[...]


def _outproj_ln(a, w, x2, g, bias):
    t, width = a.shape
    tm = WIDE_ROW_TILE
    row = lambda i: (i, 0)
    return pl.pallas_call(
        _outproj_ln_kernel,
        grid=(t // tm,),
        in_specs=[pl.BlockSpec((tm, width), row), _resident(w.shape),
                  pl.BlockSpec((tm, D_MODEL), row), _resident(g.shape), _resident(bias.shape)],
        out_specs=pl.BlockSpec((tm, D_MODEL), row),
        out_shape=jax.ShapeDtypeStruct((t, D_MODEL), F32),
        compiler_params=_params("arbitrary"),
        name="outproj_ln",
    )(a, w, x2, g, bias)


def _rope_pair(u, cos_t, sin_t):
    cos = cos_t.T
    sin = sin_t.T
    tab = jnp.concatenate([cos, cos, -sin, sin], axis=1)
    lane = lax.broadcasted_iota(jnp.int32, u.shape, 1)
    t = u * tab
    return jnp.where(lane < QK_ROPE, t + pltpu.roll(t, QK_ROPE, 1), 0.0)


def _rms(x, g, eps=1e-6):
    return x * lax.rsqrt(jnp.mean(x * x, axis=-1, keepdims=True) + eps) * g


def _mla_inproj_kernel(x_ref, win_ref, wdn_ref, qg_ref, kvg_ref, cos_ref, sin_ref,
                       qn_ref, sg_ref, lat_ref, kr_ref):
    for rows in _row_slabs(x_ref.shape[0]):
        xb = x_ref[rows, :].astype(BF16)
        qn_ref[rows, :] = _rms(_dot(xb, win_ref[:, 0:Q_LORA]), qg_ref[...]).astype(BF16)
        sg_ref[rows, :] = _silu(_dot(xb, win_ref[:, Q_LORA:])).astype(BF16)
        c = _dot(xb, wdn_ref[...])
        lat_ref[rows, :] = _rms(c[:, 0:KV_LORA], kvg_ref[...]).astype(BF16)
        kr_ref[rows, :] = _rope_pair(c[:, KV_LORA:], cos_ref[:, rows], sin_ref[:, rows]).astype(BF16)


def _mla_inproj(x2, win, wdn, qg, kvg, cos_t, sin_t, seq):
    t = x2.shape[0]
    tm = WIDE_ROW_TILE
    nseq = seq // tm
    row = lambda i: (i, 0)
    pos = lambda i: (0, i % nseq)
    widths = (Q_LORA, WIDTH_B, KV_LORA, 2 * QK_ROPE)
    return pl.pallas_call(
        _mla_inproj_kernel,
        grid=(t // tm,),
        in_specs=[pl.BlockSpec((tm, D_MODEL), row), _resident(win.shape), _resident(wdn.shape),
                  _resident(qg.shape), _resident(kvg.shape),
                  pl.BlockSpec((QK_ROPE // 2, tm), pos), pl.BlockSpec((QK_ROPE // 2, tm), pos)],
        out_specs=[pl.BlockSpec((tm, w), row) for w in widths],
        out_shape=[jax.ShapeDtypeStruct((t, w), BF16) for w in widths],
        compiler_params=_params("arbitrary"),
        name="mla_inproj",
    )(x2, win, wdn, qg, kvg, cos_t, sin_t)


def _q_up_kernel(qn_ref, wnt_ref, wrt_ref, cos_ref, sin_ref, q_ref):
    scale = (QK_NOPE + QK_ROPE) ** -0.5 * math.log2(math.e)
    tq = ATTN_TQ
    half = QK_ROPE // 2
    zeros = jnp.zeros((QK_PAD - QK_NOPE - QK_ROPE, tq), BF16)
    for blk in range(qn_ref.shape[0] // tq):
        cols = slice(blk * tq, (blk + 1) * tq)
        qn = qn_ref[cols, :]
        cos = cos_ref[:, cols]
        sin = sin_ref[:, cols]
        yt_n = _dot_nt(wnt_ref[...], qn)
        yt_r = _dot_nt(wrt_ref[...], qn)
        for h in range(H_B):
            x1 = yt_r[h * QK_ROPE:h * QK_ROPE + half]
            x2 = yt_r[h * QK_ROPE + half:(h + 1) * QK_ROPE]
            q_ref[0, h, blk, 0:QK_NOPE, :] = (yt_n[h * QK_NOPE:(h + 1) * QK_NOPE] * scale).astype(BF16)
            q_ref[0, h, blk, QK_NOPE:QK_NOPE + half, :] = ((x1 * cos - x2 * sin) * scale).astype(BF16)
            q_ref[0, h, blk, QK_NOPE + half:QK_NOPE + QK_ROPE, :] = ((x2 * cos + x1 * sin) * scale).astype(BF16)
            q_ref[0, h, blk, QK_NOPE + QK_ROPE:, :] = zeros


def _q_up(qn, wnt, wrt, cos_t, sin_t, batch, seq):
    tm = WIDE_ROW_TILE
    tq = ATTN_TQ
    nseq = seq // tm
    pos = lambda b, i: (0, i)
    return pl.pallas_call(
        _q_up_kernel,
        grid=(batch, nseq),
        in_specs=[pl.BlockSpec((tm, Q_LORA), lambda b, i: (b * nseq + i, 0)),
                  _resident(wnt.shape), _resident(wrt.shape),
                  pl.BlockSpec((QK_ROPE // 2, tm), pos), pl.BlockSpec((QK_ROPE // 2, tm), pos)],
        out_specs=pl.BlockSpec((1, H_B, tm // tq, QK_PAD, tq), lambda b, i: (b, 0, i, 0, 0)),
        out_shape=jax.ShapeDtypeStruct((batch, H_B, seq // tq, QK_PAD, tq), BF16),
        compiler_params=_params("arbitrary", "arbitrary"),
        name="q_up",
    )(qn, wnt, wrt, cos_t, sin_t)


def _kv_up_kernel(lat_ref, kr_ref, wk_ref, wvt_ref, k_ref, vt_ref):
    tk = ATTN_TK
    for blk in range(lat_ref.shape[0] // tk):
        rows = slice(blk * tk, (blk + 1) * tk)
        lat = lat_ref[rows, :]
        kn = _dot(lat, wk_ref[...]).astype(BF16)
        kr = kr_ref[rows, :]
        for h in range(H_B):
            k_ref[0, h, rows, 0:QK_NOPE] = kn[:, h * QK_NOPE:(h + 1) * QK_NOPE]
            k_ref[0, h, rows, QK_NOPE:] = kr
        vt = _dot_nt(wvt_ref[...], lat).astype(BF16)
        vt_ref[0, :, blk, 0:V_HEAD, :] = vt.reshape(H_B, V_HEAD, tk)
        extra = lax.broadcasted_iota(jnp.int32, (H_B, V_ROWS - V_HEAD, tk), 1) == 0
        vt_ref[0, :, blk, V_HEAD:, :] = extra.astype(BF16)


def _kv_up(lat, kr, wk, wvt, batch, seq):
    tk = ATTN_TK
    tm = WIDE_ROW_TILE
    nk = seq // tk
    nstep = seq // tm
    row = lambda b, i: (b * nstep + i, 0)
    return pl.pallas_call(
        _kv_up_kernel,
        grid=(batch, nstep),
        in_specs=[pl.BlockSpec((tm, KV_LORA), row), pl.BlockSpec((tm, 2 * QK_ROPE), row),
                  _resident(wk.shape), _resident(wvt.shape)],
        out_specs=[pl.BlockSpec((1, H_B, tm, QK_PAD), lambda b, i: (b, 0, i, 0)),
                   pl.BlockSpec((1, H_B, tm // tk, V_ROWS, tk), lambda b, i: (b, 0, i, 0, 0))],
        out_shape=[jax.ShapeDtypeStruct((batch, H_B, seq, QK_PAD), BF16),
                   jax.ShapeDtypeStruct((batch, H_B, nk, V_ROWS, tk), BF16)],
        compiler_params=_params("arbitrary", "arbitrary"),
        name="kv_up",
    )(lat, kr, wk, wvt)


def _attn_kernel(q_ref, k_ref, vt_ref, sg_ref, o_ref, sa_ref, sb_ref, mxa_ref, mxb_ref, m_ref, acc_ref,
                 hold_ref):
    tq = ATTN_TQ
    tk = ATTN_TK
    npairs = q_ref.shape[2] // 2
    last = npairs - 1

    def q_rows(pair, row):
        return q_ref[0, 0, 2 * pair + row]

    def scores(q, j, nblk, s_ref, mx_ref, off=0):
        kb = k_ref[0, 0, pl.ds(pl.multiple_of(j * tk, tk), nblk * tk), :]
        s = _dot(kb, q)
        s_ref[off:off + nblk * tk, :] = s
        mx_ref[...] = jnp.max(s, axis=0, keepdims=True)

    def update(row, j, nblk, s_ref, mx_ref, diag_at=None, off=0):
        s = s_ref[off:off + nblk * tk, :]
        if diag_at is None:
            block_max = mx_ref[...]
        else:
            kpos = lax.broadcasted_iota(jnp.int32, s.shape, 0)
            qpos = lax.broadcasted_iota(jnp.int32, s.shape, 1)
            s = jnp.where(kpos <= qpos + diag_at, s, MASK_VALUE)
            block_max = jnp.max(s, axis=0, keepdims=True)
        m = m_ref[row]
        m_new = jnp.maximum(m, block_max)
        alpha = jnp.exp2(m - m_new)
        p = jnp.exp2(s - m_new).astype(BF16)
        m_ref[row] = m_new
        pv = _dot(vt_ref[0, 0, j], p[0:tk])
        for i in range(1, nblk):
            pv += _dot(vt_ref[0, 0, j + i], p[i * tk:(i + 1) * tk])
        acc_ref[row] = alpha * acc_ref[row] + pv

    def finalize(pair, row, acc):
        o = (acc[0:V_HEAD, :] / acc[V_HEAD:V_HEAD + 1, :]).T
        rows = pl.ds(pl.multiple_of((2 * pair + row) * tq, tq), tq)
        o_ref[0, rows, :] = (o * sg_ref[0, rows, :].astype(F32)).astype(BF16)

    def row_pair(r, carry):
        m_ref[...] = jnp.full_like(m_ref, MASK_VALUE)
        acc_ref[...] = jnp.zeros_like(acc_ref)

        @pl.when(r == 0)
        def _():
            scores(q_rows(r, 0), 0, 1, sa_ref, mxa_ref, off=tk)
            hold_ref[...] = jnp.ones_like(hold_ref)

        @pl.when(r > 0)
        def _():
            def step(t):
                j = 2 * t
                scores(q_rows(r, 1), j, 2, sb_ref, mxb_ref)
                update(0, j, 2, sa_ref, mxa_ref)
                scores(q_rows(r, 0), jnp.minimum(j + 2, 2 * r - 1), 2, sa_ref, mxa_ref)
                update(1, j, 2, sb_ref, mxb_ref)

            def body(i, carry):
                step(2 * i)
                step(2 * i + 1)
                return carry

            lax.fori_loop(0, r // 2, body, 0)

            @pl.when(r % 2 == 1)
            def _():
                step(r - 1)

        finalize(jnp.maximum(r - 1, 0), 1, hold_ref[...])
        scores(q_rows(r, 1), 2 * r, 2, sb_ref, mxb_ref)
        update(0, 2 * r, 1, sa_ref, mxa_ref, diag_at=0, off=tk)
        scores(q_rows(jnp.minimum(r + 1, last), 0), 0, 2, sa_ref, mxa_ref)
        update(1, 2 * r, 2, sb_ref, mxb_ref, diag_at=tk)
        finalize(r, 0, acc_ref[0])
        hold_ref[...] = acc_ref[1]
        return carry

    lax.fori_loop(0, npairs, row_pair, 0)
    finalize(last, 1, hold_ref[...])


def _attention(q, k, vt, sg3):
    b, h, s, _ = k.shape
    tq = ATTN_TQ
    assert tq == ATTN_TK
    nk = s // ATTN_TK
    head = lambda bi, hi: (bi, hi, 0, 0)
    oblk = lambda bi, hi: (bi, 0, hi)
    return pl.pallas_call(
        _attn_kernel,
        grid=(b, h),
        in_specs=[pl.BlockSpec((1, 1, s // tq, QK_PAD, tq), lambda bi, hi: (bi, hi, 0, 0, 0)),
                  pl.BlockSpec((1, 1, s, QK_PAD), head),
                  pl.BlockSpec((1, 1, nk, V_ROWS, ATTN_TK), lambda bi, hi: (bi, hi, 0, 0, 0)),
                  pl.BlockSpec((1, s, V_HEAD), oblk)],
        out_specs=pl.BlockSpec((1, s, V_HEAD), oblk),
        out_shape=jax.ShapeDtypeStruct((b, s, WIDTH_B), BF16),
        scratch_shapes=[pltpu.VMEM((2 * ATTN_TK, tq), F32), pltpu.VMEM((2 * ATTN_TK, tq), F32),
                        pltpu.VMEM((1, tq), F32), pltpu.VMEM((1, tq), F32),
                        pltpu.VMEM((2, 1, tq), F32), pltpu.VMEM((2, V_ROWS, tq), F32),
                        pltpu.VMEM((V_ROWS, tq), F32)],
        compiler_params=_params("arbitrary", "arbitrary"),
        name="mla_attention",
    )(q, k, vt, sg3)


def _rope_angles(seq, half, base):
    inv = base ** (-jnp.arange(half, dtype=F32) / half)
    return jnp.arange(seq, dtype=F32)[:, None] * inv[None, :]


def _retention_decays(c, rows):
    lg = jnp.log1p(-jnp.exp2(-5.0 - jnp.arange(H_A, dtype=F32)))
    idx = jnp.arange(c, dtype=F32)
    up = jnp.exp((idx + 1.0)[:, None] * lg[None, :])
    down = jnp.exp(-(idx + 1.0)[:, None] * lg[None, :]) * DK_A ** -0.5
    widen = lambda t: jnp.tile(jnp.repeat(t, DK_A, axis=1), (rows // c, 1))
    causal = (idx[:, None] >= idx[None, :]).astype(F32)
    cdec = jnp.broadcast_to(jnp.exp(c * lg)[:, None, None], (H_A, 1, DV_A))
    return widen(up), widen(down), causal, cdec


def _swap_halves(w):
    half = w.shape[-1] // 2
    return jnp.concatenate([w[..., half:], w[..., :half]], axis=-1)


def kernel(x, a_w_in, a_w_out, b_w_in, b_q_norm, b_w_uq, b_w_out, kv_w_down, kv_norm, kv_w_up,
           ln_g, ln_b):
    batch, seq, _ = x.shape
    t = batch * seq
    x2 = x.reshape(t, D_MODEL)

    ang_a = _rope_angles(seq, DK_A // 2, ROPE_BASE_A)
    cos_a, sin_a = jnp.cos(ang_a), jnp.sin(ang_a)
    ang_t = _rope_angles(seq, QK_ROPE // 2, ROPE_BASE_B).T
    cos_t, sin_t = jnp.cos(ang_t), jnp.sin(ang_t)

    qdec, kdec, causal, cdec = _retention_decays(RET_CHUNK, ROW_TILE)
    q, k, v, sg = _ret_inproj(x2, a_w_in[0].astype(BF16), cos_a, sin_a, qdec, kdec, seq)
    shape3 = lambda a: a.reshape(batch, seq, a.shape[-1])
    o = _retention(shape3(q), shape3(k), shape3(v), shape3(sg), causal, cdec)
    x1 = _outproj_ln(o.reshape(t, WIDTH_A), a_w_out[0].astype(BF16), x2,
                     ln_g[0][None, :], ln_b[0][None, :])

    w_lat = kv_w_down[:, :KV_LORA].astype(BF16)
    w_rope = kv_w_down[:, KV_LORA:]
    w_rope = jnp.concatenate([w_rope, _swap_halves(w_rope)], axis=-1).astype(BF16)
    wdn = jnp.concatenate([w_lat, w_rope], axis=-1)
    qn, sgb, lat, kr = _mla_inproj(x1, b_w_in[0].astype(BF16), wdn,
                                   b_q_norm[0][None, :], kv_norm[None, :], cos_t, sin_t, seq)

    wuq_t = b_w_uq[0].T.reshape(H_B, QK_NOPE + QK_ROPE, Q_LORA)
    wuq_nt = wuq_t[:, :QK_NOPE].reshape(H_B * QK_NOPE, Q_LORA).astype(BF16)
    wuq_rt = wuq_t[:, QK_NOPE:].reshape(H_B * QK_ROPE, Q_LORA).astype(BF16)
    qh = _q_up(qn, wuq_nt, wuq_rt, cos_t, sin_t, batch, seq)

    wup = kv_w_up.reshape(KV_LORA, H_B, QK_NOPE + V_HEAD)
    wk = wup[..., :QK_NOPE].reshape(KV_LORA, H_B * QK_NOPE).astype(BF16)
    wvt = wup[..., QK_NOPE:].reshape(KV_LORA, H_B * V_HEAD).T.astype(BF16)
    kh, vt = _kv_up(lat, kr, wk, wvt, batch, seq)

    ob = _attention(qh, kh, vt, sgb.reshape(batch, seq, WIDTH_B))
    out = _outproj_ln(ob.reshape(t, WIDTH_B), b_w_out[0].astype(BF16), x1,
                      ln_g[1][None, :], ln_b[1][None, :])
    return out.reshape(batch, seq, D_MODEL)
```

```python
import functools
import math

import jax
import jax.numpy as jnp
from jax import lax
from jax.experimental import pallas as pl
from jax.experimental.pallas import tpu as pltpu

D_MODEL = 1024
DEPTH = 2

H_A = 4
DK_A = D_MODEL // H_A
DV_A = 2 * DK_A
WIDTH_A = H_A * DV_A
ROPE_BASE_A = 10000.0

H_B = 16
QK_NOPE = 128
QK_ROPE = 64
V_HEAD = 128
Q_LORA = 768
KV_LORA = 512
WIDTH_B = H_B * V_HEAD
ROPE_BASE_B = 10000.0
QK_PAD = 256
BF16_SUBLANE_TILE = 16
V_ROWS = V_HEAD + BF16_SUBLANE_TILE

DEEPNORM_ALPHA = (2.0 * DEPTH) ** 0.25

RET_CHUNK = 256
ROW_TILE = 512
WIDE_ROW_TILE = 1024
ROW_SLAB = 256
ATTN_TQ = 512
ATTN_TK = 512
MASK_VALUE = -1e30

VMEM_LIMIT_BYTES = 56 * 1024 * 1024

BF16 = jnp.bfloat16
F32 = jnp.float32


def _params(*semantics):
    return pltpu.CompilerParams(dimension_semantics=semantics,
                                vmem_limit_bytes=VMEM_LIMIT_BYTES)


def _resident(shape):
    zeros = (0,) * len(shape)
    return pl.BlockSpec(shape, lambda *_: zeros, pipeline_mode=pl.Buffered(1))


def _dot(a, b):
    return jnp.dot(a, b, preferred_element_type=F32)


def _dot_nt(a, b):
    return lax.dot_general(a, b, (((1,), (1,)), ((), ())), preferred_element_type=F32)


def _dot_tn(a, b):
    return lax.dot_general(a, b, (((0,), (0,)), ((), ())), preferred_element_type=F32)


def _row_slabs(rows):
    return [slice(i, i + ROW_SLAB) for i in range(0, rows, ROW_SLAB)]


def _silu(g):
    return g / (1.0 + jnp.exp(-g))


def _retention_kernel(x_ref, w_ref, cos_ref, sin_ref, qdec_ref, kdec_ref, causal_ref, cdec_ref,
                      o_ref, state_ref, sbf_ref, *, steps_per_seq):
    assert ROW_SLAB == RET_CHUNK
    qk = H_A * DK_A
    half = DK_A // 2

    @pl.when(pl.program_id(0) % steps_per_seq == 0)
    def _():
        state_ref[...] = jnp.zeros_like(state_ref)
        sbf_ref[...] = jnp.zeros_like(sbf_ref)

    def rope(h, i, rows, cos, sin, dec_ref):
        lo = slice(i * DK_A, i * DK_A + half)
        hi = slice(i * DK_A + half, (i + 1) * DK_A)
        x1 = h[:, lo]
        x2 = h[:, hi]
        r1 = ((x1 * cos - x2 * sin) * dec_ref[rows, lo]).astype(BF16)
        r2 = ((x2 * cos + x1 * sin) * dec_ref[rows, hi]).astype(BF16)
        return jnp.concatenate([r1, r2], axis=1)

    for rows in _row_slabs(x_ref.shape[0]):
        xb = x_ref[rows, :].astype(BF16)
        cos = cos_ref[rows, :]
        sin = sin_ref[rows, :]
        hq = _dot(xb, w_ref[:, 0:qk])
        hk = _dot(xb, w_ref[:, qk:2 * qk])
        v_all = _dot(xb, w_ref[:, 2 * qk:2 * qk + WIDTH_A]).astype(BF16)
        sg_all = _silu(_dot(xb, w_ref[:, 2 * qk + WIDTH_A:]))
        for h in range(H_A):
            v_cols = slice(h * DV_A, (h + 1) * DV_A)
            q = rope(hq, h, rows, cos, sin, qdec_ref)
            k = rope(hk, h, rows, cos, sin, kdec_ref)
            v = v_all[:, v_cols]
            scores = (_dot_nt(q, k) * causal_ref[...]).astype(BF16)
            o = _dot(scores, v) + _dot(q, sbf_ref[h])
            state = (state_ref[h] + _dot_tn(k, v)) * cdec_ref[h]
            state_ref[h] = state
            sbf_ref[h] = state.astype(BF16)

            mu = jnp.mean(o, axis=-1, keepdims=True)
            d = o - mu
            var = jnp.mean(d * d, axis=-1, keepdims=True)
            o_ref[rows, v_cols] = (d * lax.rsqrt(var + 1e-5) * sg_all[:, v_cols]).astype(BF16)


def _retention_layer(x2, w, cos, sin, qdec, kdec, causal, cdec, seq):
    t = x2.shape[0]
    tm = ROW_TILE
    nseq = seq // tm
    row = lambda i: (i, 0)
    pos = lambda i: (i % nseq, 0)
    return pl.pallas_call(
        functools.partial(_retention_kernel, steps_per_seq=nseq),
        grid=(t // tm,),
        in_specs=[pl.BlockSpec((tm, D_MODEL), row),
                  _resident(w.shape),
                  pl.BlockSpec((tm, DK_A // 2), pos),
                  pl.BlockSpec((tm, DK_A // 2), pos),
                  _resident(qdec.shape), _resident(kdec.shape),
                  _resident(causal.shape), _resident(cdec.shape)],
        out_specs=pl.BlockSpec((tm, WIDTH_A), row),
        out_shape=jax.ShapeDtypeStruct((t, WIDTH_A), BF16),
        scratch_shapes=[pltpu.VMEM((H_A, DK_A, DV_A), F32), pltpu.VMEM((H_A, DK_A, DV_A), BF16)],
        compiler_params=_params("arbitrary"),
        name="retention_layer",
    )(x2, w, cos, sin, qdec, kdec, causal, cdec)


def _outproj_ln_kernel(a_ref, w_ref, x_ref, g_ref, b_ref, o_ref):
    for rows in _row_slabs(a_ref.shape[0]):
        z = DEEPNORM_ALPHA * x_ref[rows, :] + _dot(a_ref[rows, :], w_ref[...])
        mu = jnp.mean(z, axis=-1, keepdims=True)
        d = z - mu
        var = jnp.mean(d * d, axis=-1, keepdims=True)
        o_ref[rows, :] = d * lax.rsqrt(var + 1e-5) * g_ref[...] + b_ref[...]


def _outproj_ln(a, w, x2, g, bias):
    t, width = a.shape
    tm = WIDE_ROW_TILE
    row = lambda i: (i, 0)
    return pl.pallas_call(
        _outproj_ln_kernel,
        grid=(t // tm,),
        in_specs=[pl.BlockSpec((tm, width), row), _resident(w.shape),
                  pl.BlockSpec((tm, D_MODEL), row), _resident(g.shape), _resident(bias.shape)],
        out_specs=pl.BlockSpec((tm, D_MODEL), row),
        out_shape=jax.ShapeDtypeStruct((t, D_MODEL), F32),
        compiler_params=_params("arbitrary"),
        name="outproj_ln",
    )(a, w, x2, g, bias)


def _rope_pair(u, cos_t, sin_t):
    cos = cos_t.T
    sin = sin_t.T
    tab = jnp.concatenate([cos, cos, -sin, sin], axis=1)
    lane = lax.broadcasted_iota(jnp.int32, u.shape, 1)
    t = u * tab
    return jnp.where(lane < QK_ROPE, t + pltpu.roll(t, QK_ROPE, 1), 0.0)


def _rms(x, g, eps=1e-6):
    return x * lax.rsqrt(jnp.mean(x * x, axis=-1, keepdims=True) + eps) * g


def _mla_inproj_kernel(x_ref, win_ref, wdn_ref, qg_ref, kvg_ref, cos_ref, sin_ref,
                       qn_ref, sg_ref, lat_ref, kr_ref):
    for rows in _row_slabs(x_ref.shape[0]):
        xb = x_ref[rows, :].astype(BF16)
        qn_ref[rows, :] = _rms(_dot(xb, win_ref[:, 0:Q_LORA]), qg_ref[...]).astype(BF16)
        sg_ref[rows, :] = _silu(_dot(xb, win_ref[:, Q_LORA:])).astype(BF16)
        c = _dot(xb, wdn_ref[...])
        lat_ref[rows, :] = _rms(c[:, 0:KV_LORA], kvg_ref[...]).astype(BF16)
        kr_ref[rows, :] = _rope_pair(c[:, KV_LORA:], cos_ref[:, rows], sin_ref[:, rows]).astype(BF16)


def _mla_inproj(x2, win, wdn, qg, kvg, cos_t, sin_t, seq):
    t = x2.shape[0]
    tm = WIDE_ROW_TILE
    nseq = seq // tm
    row = lambda i: (i, 0)
    pos = lambda i: (0, i % nseq)
    widths = (Q_LORA, WIDTH_B, KV_LORA, 2 * QK_ROPE)
    return pl.pallas_call(
        _mla_inproj_kernel,
        grid=(t // tm,),
        in_specs=[pl.BlockSpec((tm, D_MODEL), row), _resident(win.shape), _resident(wdn.shape),
                  _resident(qg.shape), _resident(kvg.shape),
                  pl.BlockSpec((QK_ROPE // 2, tm), pos), pl.BlockSpec((QK_ROPE // 2, tm), pos)],
        out_specs=[pl.BlockSpec((tm, w), row) for w in widths],
        out_shape=[jax.ShapeDtypeStruct((t, w), BF16) for w in widths],
        compiler_params=_params("arbitrary"),
        name="mla_inproj",
    )(x2, win, wdn, qg, kvg, cos_t, sin_t)


def _q_up_kernel(qn_ref, wnt_ref, wrt_ref, cos_ref, sin_ref, q_ref):
    scale = (QK_NOPE + QK_ROPE) ** -0.5 * math.log2(math.e)
    tq = ATTN_TQ
    half = QK_ROPE // 2
    zeros = jnp.zeros((QK_PAD - QK_NOPE - QK_ROPE, tq), BF16)
    for blk in range(qn_ref.shape[0] // tq):
        cols = slice(blk * tq, (blk + 1) * tq)
        qn = qn_ref[cols, :]
        cos = cos_ref[:, cols]
        sin = sin_ref[:, cols]
        yt_n = _dot_nt(wnt_ref[...], qn)
        yt_r = _dot_nt(wrt_ref[...], qn)
        for h in range(H_B):
            x1 = yt_r[h * QK_ROPE:h * QK_ROPE + half]
            x2 = yt_r[h * QK_ROPE + half:(h + 1) * QK_ROPE]
            q_ref[0, h, blk, 0:QK_NOPE, :] = (yt_n[h * QK_NOPE:(h + 1) * QK_NOPE] * scale).astype(BF16)
            q_ref[0, h, blk, QK_NOPE:QK_NOPE + half, :] = ((x1 * cos - x2 * sin) * scale).astype(BF16)
            q_ref[0, h, blk, QK_NOPE + half:QK_NOPE + QK_ROPE, :] = ((x2 * cos + x1 * sin) * scale).astype(BF16)
            q_ref[0, h, blk, QK_NOPE + QK_ROPE:, :] = zeros


def _q_up(qn, wnt, wrt, cos_t, sin_t, batch, seq):
    tm = WIDE_ROW_TILE
    tq = ATTN_TQ
    nseq = seq // tm
    pos = lambda b, i: (0, i)
    return pl.pallas_call(
        _q_up_kernel,
        grid=(batch, nseq),
        in_specs=[pl.BlockSpec((tm, Q_LORA), lambda b, i: (b * nseq + i, 0)),
                  _resident(wnt.shape), _resident(wrt.shape),
                  pl.BlockSpec((QK_ROPE // 2, tm), pos), pl.BlockSpec((QK_ROPE // 2, tm), pos)],
        out_specs=pl.BlockSpec((1, H_B, tm // tq, QK_PAD, tq), lambda b, i: (b, 0, i, 0, 0)),
        out_shape=jax.ShapeDtypeStruct((batch, H_B, seq // tq, QK_PAD, tq), BF16),
        compiler_params=_params("arbitrary", "arbitrary"),
        name="q_up",
    )(qn, wnt, wrt, cos_t, sin_t)


def _kv_up_kernel(lat_ref, kr_ref, wk_ref, wvt_ref, k_ref, vt_ref):
    tk = ATTN_TK
    for blk in range(lat_ref.shape[0] // tk):
        rows = slice(blk * tk, (blk + 1) * tk)
        lat = lat_ref[rows, :]
        kn = _dot(lat, wk_ref[...]).astype(BF16)
        kr = kr_ref[rows, :]
        for h in range(H_B):
            k_ref[0, h, rows, 0:QK_NOPE] = kn[:, h * QK_NOPE:(h + 1) * QK_NOPE]
            k_ref[0, h, rows, QK_NOPE:] = kr
        vt = _dot_nt(wvt_ref[...], lat).astype(BF16)
        vt_ref[0, :, blk, 0:V_HEAD, :] = vt.reshape(H_B, V_HEAD, tk)
        extra = lax.broadcasted_iota(jnp.int32, (H_B, V_ROWS - V_HEAD, tk), 1) == 0
        vt_ref[0, :, blk, V_HEAD:, :] = extra.astype(BF16)


def _kv_up(lat, kr, wk, wvt, batch, seq):
    tk = ATTN_TK
    tm = WIDE_ROW_TILE
    nk = seq // tk
    nstep = seq // tm
    row = lambda b, i: (b * nstep + i, 0)
    return pl.pallas_call(
        _kv_up_kernel,
        grid=(batch, nstep),
        in_specs=[pl.BlockSpec((tm, KV_LORA), row), pl.BlockSpec((tm, 2 * QK_ROPE), row),
                  _resident(wk.shape), _resident(wvt.shape)],
        out_specs=[pl.BlockSpec((1, H_B, tm, QK_PAD), lambda b, i: (b, 0, i, 0)),
                   pl.BlockSpec((1, H_B, tm // tk, V_ROWS, tk), lambda b, i: (b, 0, i, 0, 0))],
        out_shape=[jax.ShapeDtypeStruct((batch, H_B, seq, QK_PAD), BF16),
                   jax.ShapeDtypeStruct((batch, H_B, nk, V_ROWS, tk), BF16)],
        compiler_params=_params("arbitrary", "arbitrary"),
        name="kv_up",
    )(lat, kr, wk, wvt)


def _attn_kernel(q_ref, k_ref, vt_ref, sg_ref, o_ref, sa_ref, sb_ref, mxa_ref, mxb_ref, m_ref, acc_ref,
                 hold_ref):
    tq = ATTN_TQ
    tk = ATTN_TK
    npairs = q_ref.shape[2] // 2
    last = npairs - 1

    def q_rows(pair, row):
        return q_ref[0, 0, 2 * pair + row]

    def scores(q, j, nblk, s_ref, mx_ref, off=0):
        kb = k_ref[0, 0, pl.ds(pl.multiple_of(j * tk, tk), nblk * tk), :]
        s = _dot(kb, q)
        s_ref[off:off + nblk * tk, :] = s
        mx_ref[...] = jnp.max(s, axis=0, keepdims=True)

    def update(row, j, nblk, s_ref, mx_ref, diag_at=None, off=0):
        s = s_ref[off:off + nblk * tk, :]
        if diag_at is None:
            block_max = mx_ref[...]
        else:
            kpos = lax.broadcasted_iota(jnp.int32, s.shape, 0)
            qpos = lax.broadcasted_iota(jnp.int32, s.shape, 1)
            s = jnp.where(kpos <= qpos + diag_at, s, MASK_VALUE)
            block_max = jnp.max(s, axis=0, keepdims=True)
        m = m_ref[row]
        m_new = jnp.maximum(m, block_max)
        alpha = jnp.exp2(m - m_new)
        p = jnp.exp2(s - m_new).astype(BF16)
        m_ref[row] = m_new
        pv = _dot(vt_ref[0, 0, j], p[0:tk])
        for i in range(1, nblk):
            pv += _dot(vt_ref[0, 0, j + i], p[i * tk:(i + 1) * tk])
        acc_ref[row] = alpha * acc_ref[row] + pv

    def finalize(pair, row, acc):
        o = (acc[0:V_HEAD, :] / acc[V_HEAD:V_HEAD + 1, :]).T
        rows = pl.ds(pl.multiple_of((2 * pair + row) * tq, tq), tq)
        o_ref[0, rows, :] = (o * sg_ref[0, rows, :].astype(F32)).astype(BF16)

    def row_pair(r, carry):
        m_ref[...] = jnp.full_like(m_ref, MASK_VALUE)
        acc_ref[...] = jnp.zeros_like(acc_ref)

        @pl.when(r == 0)
        def _():
            scores(q_rows(r, 0), 0, 1, sa_ref, mxa_ref, off=tk)
            hold_ref[...] = jnp.ones_like(hold_ref)

        @pl.when(r > 0)
        def _():
            def step(t):
                j = 2 * t
                scores(q_rows(r, 1), j, 2, sb_ref, mxb_ref)
                update(0, j, 2, sa_ref, mxa_ref)
                scores(q_rows(r, 0), jnp.minimum(j + 2, 2 * r - 1), 2, sa_ref, mxa_ref)
                update(1, j, 2, sb_ref, mxb_ref)

            def body(i, carry):
                step(2 * i)
                step(2 * i + 1)
                return carry

            lax.fori_loop(0, r // 2, body, 0)

            @pl.when(r % 2 == 1)
            def _():
                step(r - 1)

        finalize(jnp.maximum(r - 1, 0), 1, hold_ref[...])
        scores(q_rows(r, 1), 2 * r, 2, sb_ref, mxb_ref)
        update(0, 2 * r, 1, sa_ref, mxa_ref, diag_at=0, off=tk)
        scores(q_rows(jnp.minimum(r + 1, last), 0), 0, 2, sa_ref, mxa_ref)
        update(1, 2 * r, 2, sb_ref, mxb_ref, diag_at=tk)
        finalize(r, 0, acc_ref[0])
        hold_ref[...] = acc_ref[1]
        return carry

    lax.fori_loop(0, npairs, row_pair, 0)
    finalize(last, 1, hold_ref[...])


def _attention(q, k, vt, sg3):
    b, h, s, _ = k.shape
    tq = ATTN_TQ
    assert tq == ATTN_TK
    nk = s // ATTN_TK
    head = lambda bi, hi: (bi, hi, 0, 0)
    oblk = lambda bi, hi: (bi, 0, hi)
    return pl.pallas_call(
        _attn_kernel,
        grid=(b, h),
        in_specs=[pl.BlockSpec((1, 1, s // tq, QK_PAD, tq), lambda bi, hi: (bi, hi, 0, 0, 0)),
                  pl.BlockSpec((1, 1, s, QK_PAD), head),
                  pl.BlockSpec((1, 1, nk, V_ROWS, ATTN_TK), lambda bi, hi: (bi, hi, 0, 0, 0)),
                  pl.BlockSpec((1, s, V_HEAD), oblk)],
        out_specs=pl.BlockSpec((1, s, V_HEAD), oblk),
        out_shape=jax.ShapeDtypeStruct((b, s, WIDTH_B), BF16),
        scratch_shapes=[pltpu.VMEM((2 * ATTN_TK, tq), F32), pltpu.VMEM((2 * ATTN_TK, tq), F32),
                        pltpu.VMEM((1, tq), F32), pltpu.VMEM((1, tq), F32),
                        pltpu.VMEM((2, 1, tq), F32), pltpu.VMEM((2, V_ROWS, tq), F32),
                        pltpu.VMEM((V_ROWS, tq), F32)],
        compiler_params=_params("arbitrary", "arbitrary"),
        name="mla_attention",
    )(q, k, vt, sg3)


def _rope_angles(seq, half, base):
    inv = base ** (-jnp.arange(half, dtype=F32) / half)
    return jnp.arange(seq, dtype=F32)[:, None] * inv[None, :]


def _retention_decays(c, rows):
    lg = jnp.log1p(-jnp.exp2(-5.0 - jnp.arange(H_A, dtype=F32)))
    idx = jnp.arange(c, dtype=F32)
    up = jnp.exp((idx + 1.0)[:, None] * lg[None, :])
    down = jnp.exp(-(idx + 1.0)[:, None] * lg[None, :]) * DK_A ** -0.5
    widen = lambda t: jnp.tile(jnp.repeat(t, DK_A, axis=1), (rows // c, 1))
    causal = (idx[:, None] >= idx[None, :]).astype(F32)
    cdec = jnp.broadcast_to(jnp.exp(c * lg)[:, None, None], (H_A, 1, DV_A))
    return widen(up), widen(down), causal, cdec


def _swap_halves(w):
    half = w.shape[-1] // 2
    return jnp.concatenate([w[..., half:], w[..., :half]], axis=-1)


def kernel(x, a_w_in, a_w_out, b_w_in, b_q_norm, b_w_uq, b_w_out, kv_w_down, kv_norm, kv_w_up,
           ln_g, ln_b):
    batch, seq, _ = x.shape
    t = batch * seq
    x2 = x.reshape(t, D_MODEL)

    ang_a = _rope_angles(seq, DK_A // 2, ROPE_BASE_A)
    cos_a, sin_a = jnp.cos(ang_a), jnp.sin(ang_a)
    ang_t = _rope_angles(seq, QK_ROPE // 2, ROPE_BASE_B).T
    cos_t, sin_t = jnp.cos(ang_t), jnp.sin(ang_t)

    qdec, kdec, causal, cdec = _retention_decays(RET_CHUNK, ROW_TILE)
    o = _retention_layer(x2, a_w_in[0].astype(BF16), cos_a, sin_a, qdec, kdec, causal, cdec, seq)
    x1 = _outproj_ln(o, a_w_out[0].astype(BF16), x2,
                     ln_g[0][None, :], ln_b[0][None, :])

    w_lat = kv_w_down[:, :KV_LORA].astype(BF16)
    w_rope = kv_w_down[:, KV_LORA:]
    w_rope = jnp.concatenate([w_rope, _swap_halves(w_rope)], axis=-1).astype(BF16)
    wdn = jnp.concatenate([w_lat, w_rope], axis=-1)
    qn, sgb, lat, kr = _mla_inproj(x1, b_w_in[0].astype(BF16), wdn,
                                   b_q_norm[0][None, :], kv_norm[None, :], cos_t, sin_t, seq)

    wuq_t = b_w_uq[0].T.reshape(H_B, QK_NOPE + QK_ROPE, Q_LORA)
    wuq_nt = wuq_t[:, :QK_NOPE].reshape(H_B * QK_NOPE, Q_LORA).astype(BF16)
    wuq_rt = wuq_t[:, QK_NOPE:].reshape(H_B * QK_ROPE, Q_LORA).astype(BF16)
    qh = _q_up(qn, wuq_nt, wuq_rt, cos_t, sin_t, batch, seq)

    wup = kv_w_up.reshape(KV_LORA, H_B, QK_NOPE + V_HEAD)
    wk = wup[..., :QK_NOPE].reshape(KV_LORA, H_B * QK_NOPE).astype(BF16)
    wvt = wup[..., QK_NOPE:].reshape(KV_LORA, H_B * V_HEAD).T.astype(BF16)
    kh, vt = _kv_up(lat, kr, wk, wvt, batch, seq)

    ob = _attention(qh, kh, vt, sgb.reshape(batch, seq, WIDTH_B))
    out = _outproj_ln(ob.reshape(t, WIDTH_B), b_w_out[0].astype(BF16), x1,
                      ln_g[1][None, :], ln_b[1][None, :])
    return out.reshape(batch, seq, D_MODEL)
```

```python
import functools
import math

import jax
import jax.numpy as jnp
from jax import lax
from jax.experimental import pallas as pl
from jax.experimental.pallas import tpu as pltpu

D_MODEL = 1024
DEPTH = 2

H_A = 4
DK_A = D_MODEL // H_A
DV_A = 2 * DK_A
WIDTH_A = H_A * DV_A
ROPE_BASE_A = 10000.0

H_B = 16
QK_NOPE = 128
QK_ROPE = 64
V_HEAD = 128
Q_LORA = 768
KV_LORA = 512
WIDTH_B = H_B * V_HEAD
ROPE_BASE_B = 10000.0
QK_PAD = 256
BF16_SUBLANE_TILE = 16
V_ROWS = V_HEAD + BF16_SUBLANE_TILE

DEEPNORM_ALPHA = (2.0 * DEPTH) ** 0.25

RET_CHUNK = 256
ROW_TILE = 512
WIDE_ROW_TILE = 1024
ROW_SLAB = 256
ATTN_TQ = 512
ATTN_TK = 512
MASK_VALUE = -1e30

VMEM_LIMIT_BYTES = 56 * 1024 * 1024

BF16 = jnp.bfloat16
F32 = jnp.float32


def _params(*semantics):
    return pltpu.CompilerParams(dimension_semantics=semantics,
                                vmem_limit_bytes=VMEM_LIMIT_BYTES)


def _resident(shape):
    zeros = (0,) * len(shape)
    return pl.BlockSpec(shape, lambda *_: zeros, pipeline_mode=pl.Buffered(1))


def _dot(a, b):
    return jnp.dot(a, b, preferred_element_type=F32)


def _dot_nt(a, b):
    return lax.dot_general(a, b, (((1,), (1,)), ((), ())), preferred_element_type=F32)


def _dot_tn(a, b):
    return lax.dot_general(a, b, (((0,), (0,)), ((), ())), preferred_element_type=F32)


def _row_slabs(rows):
    return [slice(i, i + ROW_SLAB) for i in range(0, rows, ROW_SLAB)]


def _silu(g):
    return g / (1.0 + jnp.exp(-g))


def _retention_kernel(x_ref, w_ref, cos_ref, sin_ref, qdec_ref, kdec_ref, causal_ref, cdec_ref,
                      o_ref, state_ref, sbf_ref, *, steps_per_seq):
    assert ROW_SLAB == RET_CHUNK
    qk = H_A * DK_A
    half = DK_A // 2

    @pl.when(pl.program_id(0) % steps_per_seq == 0)
    def _():
        state_ref[...] = jnp.zeros_like(state_ref)
        sbf_ref[...] = jnp.zeros_like(sbf_ref)

    def rope(h, i, cos, sin, dec_ref):
        lo = slice(i * DK_A, i * DK_A + half)
        hi = slice(i * DK_A + half, (i + 1) * DK_A)
        x1 = h[:, lo]
        x2 = h[:, hi]
        r1 = ((x1 * cos - x2 * sin) * dec_ref[:, lo]).astype(BF16)
        r2 = ((x2 * cos + x1 * sin) * dec_ref[:, hi]).astype(BF16)
        return jnp.concatenate([r1, r2], axis=1)

    for rows in _row_slabs(x_ref.shape[0]):
        xb = x_ref[rows, :].astype(BF16)
        cos = cos_ref[rows, :]
        sin = sin_ref[rows, :]
        hq = _dot(xb, w_ref[:, 0:qk])
        hk = _dot(xb, w_ref[:, qk:2 * qk])
        v_all = _dot(xb, w_ref[:, 2 * qk:2 * qk + WIDTH_A]).astype(BF16)
        sg_all = _silu(_dot(xb, w_ref[:, 2 * qk + WIDTH_A:]))
        for h in range(H_A):
            v_cols = slice(h * DV_A, (h + 1) * DV_A)
            q = rope(hq, h, cos, sin, qdec_ref)
            k = rope(hk, h, cos, sin, kdec_ref)
            v = v_all[:, v_cols]
            scores = (_dot_nt(q, k) * causal_ref[...]).astype(BF16)
            o = _dot(scores, v) + _dot(q, sbf_ref[h])
            state = (state_ref[h] + _dot_tn(k, v)) * cdec_ref[h]
            state_ref[h] = state
            sbf_ref[h] = state.astype(BF16)

            mu = jnp.mean(o, axis=-1, keepdims=True)
            d = o - mu
            var = jnp.mean(d * d, axis=-1, keepdims=True)
            o_ref[rows, v_cols] = (d * lax.rsqrt(var + 1e-5) * sg_all[:, v_cols]).astype(BF16)


def _retention_layer(x2, w, cos, sin, qdec, kdec, causal, cdec, seq):
    t = x2.shape[0]
    tm = WIDE_ROW_TILE
    nseq = seq // tm
    row = lambda i: (i, 0)
    pos = lambda i: (i % nseq, 0)
    return pl.pallas_call(
        functools.partial(_retention_kernel, steps_per_seq=nseq),
        grid=(t // tm,),
        in_specs=[pl.BlockSpec((tm, D_MODEL), row),
                  _resident(w.shape),
                  pl.BlockSpec((tm, DK_A // 2), pos),
                  pl.BlockSpec((tm, DK_A // 2), pos),
                  _resident(qdec.shape), _resident(kdec.shape),
                  _resident(causal.shape), _resident(cdec.shape)],
        out_specs=pl.BlockSpec((tm, WIDTH_A), row),
        out_shape=jax.ShapeDtypeStruct((t, WIDTH_A), BF16),
        scratch_shapes=[pltpu.VMEM((H_A, DK_A, DV_A), F32), pltpu.VMEM((H_A, DK_A, DV_A), BF16)],
        compiler_params=_params("arbitrary"),
        name="retention_layer",
    )(x2, w, cos, sin, qdec, kdec, causal, cdec)


def _outproj_ln_kernel(a_ref, w_ref, x_ref, g_ref, b_ref, o_ref):
    for rows in _row_slabs(a_ref.shape[0]):
        z = DEEPNORM_ALPHA * x_ref[rows, :] + _dot(a_ref[rows, :], w_ref[...])
        mu = jnp.mean(z, axis=-1, keepdims=True)
        d = z - mu
        var = jnp.mean(d * d, axis=-1, keepdims=True)
        o_ref[rows, :] = d * lax.rsqrt(var + 1e-5) * g_ref[...] + b_ref[...]


def _outproj_ln(a, w, x2, g, bias):
    t, width = a.shape
    tm = WIDE_ROW_TILE
    row = lambda i: (i, 0)
    return pl.pallas_call(
        _outproj_ln_kernel,
        grid=(t // tm,),
        in_specs=[pl.BlockSpec((tm, width), row), _resident(w.shape),
                  pl.BlockSpec((tm, D_MODEL), row), _resident(g.shape), _resident(bias.shape)],
        out_specs=pl.BlockSpec((tm, D_MODEL), row),
        out_shape=jax.ShapeDtypeStruct((t, D_MODEL), F32),
        compiler_params=_params("arbitrary"),
        name="outproj_ln",
    )(a, w, x2, g, bias)


def _rope_pair(u, cos_t, sin_t):
    cos = cos_t.T
    sin = sin_t.T
    tab = jnp.concatenate([cos, cos, -sin, sin], axis=1)
    lane = lax.broadcasted_iota(jnp.int32, u.shape, 1)
    t = u * tab
    return jnp.where(lane < QK_ROPE, t + pltpu.roll(t, QK_ROPE, 1), 0.0)


def _rms(x, g, eps=1e-6):
    return x * lax.rsqrt(jnp.mean(x * x, axis=-1, keepdims=True) + eps) * g


def _mla_inproj_kernel(x_ref, win_ref, wdn_ref, qg_ref, kvg_ref, cos_ref, sin_ref,
                       qn_ref, sg_ref, lat_ref, kr_ref):
    for rows in _row_slabs(x_ref.shape[0]):
        xb = x_ref[rows, :].astype(BF16)
        qn_ref[rows, :] = _rms(_dot(xb, win_ref[:, 0:Q_LORA]), qg_ref[...]).astype(BF16)
        sg_ref[rows, :] = _silu(_dot(xb, win_ref[:, Q_LORA:])).astype(BF16)
        c = _dot(xb, wdn_ref[...])
        lat_ref[rows, :] = _rms(c[:, 0:KV_LORA], kvg_ref[...]).astype(BF16)
        kr_ref[rows, :] = _rope_pair(c[:, KV_LORA:], cos_ref[:, rows], sin_ref[:, rows]).astype(BF16)


def _mla_inproj(x2, win, wdn, qg, kvg, cos_t, sin_t, seq):
    t = x2.shape[0]
    tm = WIDE_ROW_TILE
    nseq = seq // tm
    row = lambda i: (i, 0)
    pos = lambda i: (0, i % nseq)
    widths = (Q_LORA, WIDTH_B, KV_LORA, 2 * QK_ROPE)
    return pl.pallas_call(
        _mla_inproj_kernel,
        grid=(t // tm,),
        in_specs=[pl.BlockSpec((tm, D_MODEL), row), _resident(win.shape), _resident(wdn.shape),
                  _resident(qg.shape), _resident(kvg.shape),
                  pl.BlockSpec((QK_ROPE // 2, tm), pos), pl.BlockSpec((QK_ROPE // 2, tm), pos)],
        out_specs=[pl.BlockSpec((tm, w), row) for w in widths],
        out_shape=[jax.ShapeDtypeStruct((t, w), BF16) for w in widths],
        compiler_params=_params("arbitrary"),
        name="mla_inproj",
    )(x2, win, wdn, qg, kvg, cos_t, sin_t)


def _q_up_kernel(qn_ref, wnt_ref, wrt_ref, cos_ref, sin_ref, q_ref):
    scale = (QK_NOPE + QK_ROPE) ** -0.5 * math.log2(math.e)
    tq = ATTN_TQ
    half = QK_ROPE // 2
    zeros = jnp.zeros((QK_PAD - QK_NOPE - QK_ROPE, tq), BF16)
    for blk in range(qn_ref.shape[0] // tq):
        cols = slice(blk * tq, (blk + 1) * tq)
        qn = qn_ref[cols, :]
        cos = cos_ref[:, cols]
        sin = sin_ref[:, cols]
        yt_n = _dot_nt(wnt_ref[...], qn)
        yt_r = _dot_nt(wrt_ref[...], qn)
        for h in range(H_B):
            x1 = yt_r[h * QK_ROPE:h * QK_ROPE + half]
            x2 = yt_r[h * QK_ROPE + half:(h + 1) * QK_ROPE]
            q_ref[0, h, blk, 0:QK_NOPE, :] = (yt_n[h * QK_NOPE:(h + 1) * QK_NOPE] * scale).astype(BF16)
            q_ref[0, h, blk, QK_NOPE:QK_NOPE + half, :] = ((x1 * cos - x2 * sin) * scale).astype(BF16)
            q_ref[0, h, blk, QK_NOPE + half:QK_NOPE + QK_ROPE, :] = ((x2 * cos + x1 * sin) * scale).astype(BF16)
            q_ref[0, h, blk, QK_NOPE + QK_ROPE:, :] = zeros


def _q_up(qn, wnt, wrt, cos_t, sin_t, batch, seq):
    tm = WIDE_ROW_TILE
    tq = ATTN_TQ
    nseq = seq // tm
    pos = lambda b, i: (0, i)
    return pl.pallas_call(
        _q_up_kernel,
        grid=(batch, nseq),
        in_specs=[pl.BlockSpec((tm, Q_LORA), lambda b, i: (b * nseq + i, 0)),
                  _resident(wnt.shape), _resident(wrt.shape),
                  pl.BlockSpec((QK_ROPE // 2, tm), pos), pl.BlockSpec((QK_ROPE // 2, tm), pos)],
        out_specs=pl.BlockSpec((1, H_B, tm // tq, QK_PAD, tq), lambda b, i: (b, 0, i, 0, 0)),
        out_shape=jax.ShapeDtypeStruct((batch, H_B, seq // tq, QK_PAD, tq), BF16),
        compiler_params=_params("arbitrary", "arbitrary"),
        name="q_up",
    )(qn, wnt, wrt, cos_t, sin_t)


def _kv_up_kernel(lat_ref, kr_ref, wk_ref, wvt_ref, k_ref, vt_ref):
    tk = ATTN_TK
    for blk in range(lat_ref.shape[0] // tk):
        rows = slice(blk * tk, (blk + 1) * tk)
        lat = lat_ref[rows, :]
        kn = _dot(lat, wk_ref[...]).astype(BF16)
        kr = kr_ref[rows, :]
        for h in range(H_B):
            k_ref[0, h, rows, 0:QK_NOPE] = kn[:, h * QK_NOPE:(h + 1) * QK_NOPE]
            k_ref[0, h, rows, QK_NOPE:] = kr
        vt = _dot_nt(wvt_ref[...], lat).astype(BF16)
        vt_ref[0, :, blk, 0:V_HEAD, :] = vt.reshape(H_B, V_HEAD, tk)
        extra = lax.broadcasted_iota(jnp.int32, (H_B, V_ROWS - V_HEAD, tk), 1) == 0
        vt_ref[0, :, blk, V_HEAD:, :] = extra.astype(BF16)


def _kv_up(lat, kr, wk, wvt, batch, seq):
    tk = ATTN_TK
    tm = WIDE_ROW_TILE
    nk = seq // tk
    nstep = seq // tm
    row = lambda b, i: (b * nstep + i, 0)
    return pl.pallas_call(
        _kv_up_kernel,
        grid=(batch, nstep),
        in_specs=[pl.BlockSpec((tm, KV_LORA), row), pl.BlockSpec((tm, 2 * QK_ROPE), row),
                  _resident(wk.shape), _resident(wvt.shape)],
        out_specs=[pl.BlockSpec((1, H_B, tm, QK_PAD), lambda b, i: (b, 0, i, 0)),
                   pl.BlockSpec((1, H_B, tm // tk, V_ROWS, tk), lambda b, i: (b, 0, i, 0, 0))],
        out_shape=[jax.ShapeDtypeStruct((batch, H_B, seq, QK_PAD), BF16),
                   jax.ShapeDtypeStruct((batch, H_B, nk, V_ROWS, tk), BF16)],
        compiler_params=_params("arbitrary", "arbitrary"),
        name="kv_up",
    )(lat, kr, wk, wvt)


def _attn_kernel(q_ref, k_ref, vt_ref, sg_ref, o_ref, sa_ref, sb_ref, mxa_ref, mxb_ref, m_ref, acc_ref,
                 hold_ref):
    tq = ATTN_TQ
    tk = ATTN_TK
    npairs = q_ref.shape[2] // 2
    last = npairs - 1

    def q_rows(pair, row):
        return q_ref[0, 0, 2 * pair + row]

    def scores(q, j, nblk, s_ref, mx_ref, off=0):
        kb = k_ref[0, 0, pl.ds(pl.multiple_of(j * tk, tk), nblk * tk), :]
        s = _dot(kb, q)
        s_ref[off:off + nblk * tk, :] = s
        mx_ref[...] = jnp.max(s, axis=0, keepdims=True)

    def update(row, j, nblk, s_ref, mx_ref, diag_at=None, off=0):
        s = s_ref[off:off + nblk * tk, :]
        if diag_at is None:
            block_max = mx_ref[...]
        else:
            kpos = lax.broadcasted_iota(jnp.int32, s.shape, 0)
            qpos = lax.broadcasted_iota(jnp.int32, s.shape, 1)
            s = jnp.where(kpos <= qpos + diag_at, s, MASK_VALUE)
            block_max = jnp.max(s, axis=0, keepdims=True)
        m = m_ref[row]
        m_new = jnp.maximum(m, block_max)
        alpha = jnp.exp2(m - m_new)
        p = jnp.exp2(s - m_new).astype(BF16)
        m_ref[row] = m_new
        pv = _dot(vt_ref[0, 0, j], p[0:tk])
        for i in range(1, nblk):
            pv += _dot(vt_ref[0, 0, j + i], p[i * tk:(i + 1) * tk])
        acc_ref[row] = alpha * acc_ref[row] + pv

    def finalize(pair, row, acc):
        o = (acc[0:V_HEAD, :] / acc[V_HEAD:V_HEAD + 1, :]).T
        rows = pl.ds(pl.multiple_of((2 * pair + row) * tq, tq), tq)
        o_ref[0, rows, :] = (o * sg_ref[0, rows, :].astype(F32)).astype(BF16)

    def row_pair(r, carry):
        m_ref[...] = jnp.full_like(m_ref, MASK_VALUE)
        acc_ref[...] = jnp.zeros_like(acc_ref)

        @pl.when(r == 0)
        def _():
            scores(q_rows(r, 0), 0, 1, sa_ref, mxa_ref, off=tk)
            hold_ref[...] = jnp.ones_like(hold_ref)

        @pl.when(r > 0)
        def _():
            def step(t):
                j = 2 * t
                scores(q_rows(r, 1), j, 2, sb_ref, mxb_ref)
                update(0, j, 2, sa_ref, mxa_ref)
                scores(q_rows(r, 0), jnp.minimum(j + 2, 2 * r - 1), 2, sa_ref, mxa_ref)
                update(1, j, 2, sb_ref, mxb_ref)

            def body(i, carry):
                step(2 * i)
                step(2 * i + 1)
                return carry

            lax.fori_loop(0, r // 2, body, 0)

            @pl.when(r % 2 == 1)
            def _():
                step(r - 1)

        finalize(jnp.maximum(r - 1, 0), 1, hold_ref[...])
        scores(q_rows(r, 1), 2 * r, 2, sb_ref, mxb_ref)
        update(0, 2 * r, 1, sa_ref, mxa_ref, diag_at=0, off=tk)
        scores(q_rows(jnp.minimum(r + 1, last), 0), 0, 2, sa_ref, mxa_ref)
        update(1, 2 * r, 2, sb_ref, mxb_ref, diag_at=tk)
        finalize(r, 0, acc_ref[0])
        hold_ref[...] = acc_ref[1]
        return carry

    lax.fori_loop(0, npairs, row_pair, 0)
    finalize(last, 1, hold_ref[...])


def _attention(q, k, vt, sg3):
    b, h, s, _ = k.shape
    tq = ATTN_TQ
    assert tq == ATTN_TK
    nk = s // ATTN_TK
    head = lambda bi, hi: (bi, hi, 0, 0)
    oblk = lambda bi, hi: (bi, 0, hi)
    return pl.pallas_call(
        _attn_kernel,
        grid=(b, h),
        in_specs=[pl.BlockSpec((1, 1, s // tq, QK_PAD, tq), lambda bi, hi: (bi, hi, 0, 0, 0)),
                  pl.BlockSpec((1, 1, s, QK_PAD), head),
                  pl.BlockSpec((1, 1, nk, V_ROWS, ATTN_TK), lambda bi, hi: (bi, hi, 0, 0, 0)),
                  pl.BlockSpec((1, s, V_HEAD), oblk)],
        out_specs=pl.BlockSpec((1, s, V_HEAD), oblk),
        out_shape=jax.ShapeDtypeStruct((b, s, WIDTH_B), BF16),
        scratch_shapes=[pltpu.VMEM((2 * ATTN_TK, tq), F32), pltpu.VMEM((2 * ATTN_TK, tq), F32),
                        pltpu.VMEM((1, tq), F32), pltpu.VMEM((1, tq), F32),
                        pltpu.VMEM((2, 1, tq), F32), pltpu.VMEM((2, V_ROWS, tq), F32),
                        pltpu.VMEM((V_ROWS, tq), F32)],
        compiler_params=_params("arbitrary", "arbitrary"),
        name="mla_attention",
    )(q, k, vt, sg3)


def _rope_angles(seq, half, base):
    inv = base ** (-jnp.arange(half, dtype=F32) / half)
    return jnp.arange(seq, dtype=F32)[:, None] * inv[None, :]


def _retention_decays(c):
    lg = jnp.log1p(-jnp.exp2(-5.0 - jnp.arange(H_A, dtype=F32)))
    idx = jnp.arange(c, dtype=F32)
    up = jnp.exp((idx + 1.0)[:, None] * lg[None, :])
    down = jnp.exp(-(idx + 1.0)[:, None] * lg[None, :]) * DK_A ** -0.5
    widen = lambda t: jnp.repeat(t, DK_A, axis=1)
    causal = (idx[:, None] >= idx[None, :]).astype(F32)
    cdec = jnp.broadcast_to(jnp.exp(c * lg)[:, None, None], (H_A, 1, DV_A))
    return widen(up), widen(down), causal, cdec


def _swap_halves(w):
    half = w.shape[-1] // 2
    return jnp.concatenate([w[..., half:], w[..., :half]], axis=-1)


def kernel(x, a_w_in, a_w_out, b_w_in, b_q_norm, b_w_uq, b_w_out, kv_w_down, kv_norm, kv_w_up,
           ln_g, ln_b):
    batch, seq, _ = x.shape
    t = batch * seq
    x2 = x.reshape(t, D_MODEL)

    ang_a = _rope_angles(seq, DK_A // 2, ROPE_BASE_A)
    cos_a, sin_a = jnp.cos(ang_a), jnp.sin(ang_a)
    ang_t = _rope_angles(seq, QK_ROPE // 2, ROPE_BASE_B).T
    cos_t, sin_t = jnp.cos(ang_t), jnp.sin(ang_t)

    qdec, kdec, causal, cdec = _retention_decays(RET_CHUNK)
    o = _retention_layer(x2, a_w_in[0].astype(BF16), cos_a, sin_a, qdec, kdec, causal, cdec, seq)
    x1 = _outproj_ln(o, a_w_out[0].astype(BF16), x2,
                     ln_g[0][None, :], ln_b[0][None, :])

    w_lat = kv_w_down[:, :KV_LORA].astype(BF16)
    w_rope = kv_w_down[:, KV_LORA:]
    w_rope = jnp.concatenate([w_rope, _swap_halves(w_rope)], axis=-1).astype(BF16)
    wdn = jnp.concatenate([w_lat, w_rope], axis=-1)
    qn, sgb, lat, kr = _mla_inproj(x1, b_w_in[0].astype(BF16), wdn,
                                   b_q_norm[0][None, :], kv_norm[None, :], cos_t, sin_t, seq)

    wuq_t = b_w_uq[0].T.reshape(H_B, QK_NOPE + QK_ROPE, Q_LORA)
    wuq_nt = wuq_t[:, :QK_NOPE].reshape(H_B * QK_NOPE, Q_LORA).astype(BF16)
    wuq_rt = wuq_t[:, QK_NOPE:].reshape(H_B * QK_ROPE, Q_LORA).astype(BF16)
    qh = _q_up(qn, wuq_nt, wuq_rt, cos_t, sin_t, batch, seq)

    wup = kv_w_up.reshape(KV_LORA, H_B, QK_NOPE + V_HEAD)
    wk = wup[..., :QK_NOPE].reshape(KV_LORA, H_B * QK_NOPE).astype(BF16)
    wvt = wup[..., QK_NOPE:].reshape(KV_LORA, H_B * V_HEAD).T.astype(BF16)
    kh, vt = _kv_up(lat, kr, wk, wvt, batch, seq)

    ob = _attention(qh, kh, vt, sgb.reshape(batch, seq, WIDTH_B))
    out = _outproj_ln(ob.reshape(t, WIDTH_B), b_w_out[0].astype(BF16), x1,
                      ln_g[1][None, :], ln_b[1][None, :])
    return out.reshape(batch, seq, D_MODEL)
```

```python
import functools
import math

import jax
import jax.numpy as jnp
from jax import lax
from jax.experimental import pallas as pl
from jax.experimental.pallas import tpu as pltpu

D_MODEL = 1024
DEPTH = 2

H_A = 4
DK_A = D_MODEL // H_A
DV_A = 2 * DK_A
WIDTH_A = H_A * DV_A
ROPE_BASE_A = 10000.0

H_B = 16
QK_NOPE = 128
QK_ROPE = 64
V_HEAD = 128
Q_LORA = 768
KV_LORA = 512
WIDTH_B = H_B * V_HEAD
ROPE_BASE_B = 10000.0
QK_PAD = 256
BF16_SUBLANE_TILE = 16
V_ROWS = V_HEAD + BF16_SUBLANE_TILE

DEEPNORM_ALPHA = (2.0 * DEPTH) ** 0.25

RET_CHUNK = 256
ROW_TILE = 512
WIDE_ROW_TILE = 1024
ROW_SLAB = 256
ATTN_TQ = 512
ATTN_TK = 512
MASK_VALUE = -1e30

VMEM_LIMIT_BYTES = 56 * 1024 * 1024

BF16 = jnp.bfloat16
F32 = jnp.float32


def _params(*semantics):
    return pltpu.CompilerParams(dimension_semantics=semantics,
                                vmem_limit_bytes=VMEM_LIMIT_BYTES)


def _resident(shape):
    zeros = (0,) * len(shape)
    return pl.BlockSpec(shape, lambda *_: zeros, pipeline_mode=pl.Buffered(1))


def _dot(a, b):
    return jnp.dot(a, b, preferred_element_type=F32)


def _dot_nt(a, b):
    return lax.dot_general(a, b, (((1,), (1,)), ((), ())), preferred_element_type=F32)


def _dot_tn(a, b):
    return lax.dot_general(a, b, (((0,), (0,)), ((), ())), preferred_element_type=F32)


def _row_slabs(rows):
    return [slice(i, i + ROW_SLAB) for i in range(0, rows, ROW_SLAB)]


def _silu(g):
    return g / (1.0 + jnp.exp(-g))


def _retention_kernel(x_ref, w_ref, cos_ref, sin_ref, qdec_ref, kdec_ref, causal_ref, cdec_ref,
                      o_ref, state_ref, sbf_ref, *, steps_per_seq):
    assert ROW_SLAB == RET_CHUNK
    qk = H_A * DK_A
    half = DK_A // 2

    @pl.when(pl.program_id(0) % steps_per_seq == 0)
    def _():
        state_ref[...] = jnp.zeros_like(state_ref)
        sbf_ref[...] = jnp.zeros_like(sbf_ref)

    def rope(h, i, cos, sin, dec_ref):
        lo = slice(i * DK_A, i * DK_A + half)
        hi = slice(i * DK_A + half, (i + 1) * DK_A)
        x1 = h[:, lo]
        x2 = h[:, hi]
        r1 = ((x1 * cos - x2 * sin) * dec_ref[:, lo]).astype(BF16)
        r2 = ((x2 * cos + x1 * sin) * dec_ref[:, hi]).astype(BF16)
        return jnp.concatenate([r1, r2], axis=1)

    for rows in _row_slabs(x_ref.shape[0]):
        xb = x_ref[rows, :].astype(BF16)
        cos = cos_ref[rows, :]
        sin = sin_ref[rows, :]
        hq = _dot(xb, w_ref[:, 0:qk])
        hk = _dot(xb, w_ref[:, qk:2 * qk])
        v_all = _dot(xb, w_ref[:, 2 * qk:2 * qk + WIDTH_A]).astype(BF16)
        sg_all = _silu(_dot(xb, w_ref[:, 2 * qk + WIDTH_A:]))
        for h in range(H_A):
            v_cols = slice(h * DV_A, (h + 1) * DV_A)
            q = rope(hq, h, cos, sin, qdec_ref)
            k = rope(hk, h, cos, sin, kdec_ref)
            v = v_all[:, v_cols]
            scores = (_dot_nt(q, k) * causal_ref[...]).astype(BF16)
            o = _dot(scores, v) + _dot(q, sbf_ref[h])
            state = (state_ref[h] + _dot_tn(k, v)) * cdec_ref[h]
            state_ref[h] = state
            sbf_ref[h] = state.astype(BF16)

            mu = jnp.mean(o, axis=-1, keepdims=True)
            d = o - mu
            var = jnp.mean(d * d, axis=-1, keepdims=True)
            o_ref[rows, v_cols] = (d * lax.rsqrt(var + 1e-5) * sg_all[:, v_cols]).astype(BF16)


def _retention_layer(x2, w, cos, sin, qdec, kdec, causal, cdec, seq):
    t = x2.shape[0]
    tm = WIDE_ROW_TILE
    nseq = seq // tm
    row = lambda i: (i, 0)
    pos = lambda i: (i % nseq, 0)
    return pl.pallas_call(
        functools.partial(_retention_kernel, steps_per_seq=nseq),
        grid=(t // tm,),
        in_specs=[pl.BlockSpec((tm, D_MODEL), row),
                  _resident(w.shape),
                  pl.BlockSpec((tm, DK_A // 2), pos),
                  pl.BlockSpec((tm, DK_A // 2), pos),
                  _resident(qdec.shape), _resident(kdec.shape),
                  _resident(causal.shape), _resident(cdec.shape)],
        out_specs=pl.BlockSpec((tm, WIDTH_A), row),
        out_shape=jax.ShapeDtypeStruct((t, WIDTH_A), BF16),
        scratch_shapes=[pltpu.VMEM((H_A, DK_A, DV_A), F32), pltpu.VMEM((H_A, DK_A, DV_A), BF16)],
        compiler_params=_params("arbitrary"),
        name="retention_layer",
    )(x2, w, cos, sin, qdec, kdec, causal, cdec)


def _outproj_ln_kernel(a_ref, w_ref, x_ref, g_ref, b_ref, o_ref):
    for rows in _row_slabs(a_ref.shape[0]):
        z = DEEPNORM_ALPHA * x_ref[rows, :] + _dot(a_ref[rows, :], w_ref[...])
        mu = jnp.mean(z, axis=-1, keepdims=True)
        d = z - mu
        var = jnp.mean(d * d, axis=-1, keepdims=True)
        o_ref[rows, :] = d * lax.rsqrt(var + 1e-5) * g_ref[...] + b_ref[...]


def _outproj_ln(a, w, x2, g, bias):
    t, width = a.shape
    tm = WIDE_ROW_TILE
    row = lambda i: (i, 0)
    return pl.pallas_call(
        _outproj_ln_kernel,
        grid=(t // tm,),
        in_specs=[pl.BlockSpec((tm, width), row), _resident(w.shape),
                  pl.BlockSpec((tm, D_MODEL), row), _resident(g.shape), _resident(bias.shape)],
        out_specs=pl.BlockSpec((tm, D_MODEL), row),
        out_shape=jax.ShapeDtypeStruct((t, D_MODEL), F32),
        compiler_params=_params("arbitrary"),
        name="outproj_ln",
    )(a, w, x2, g, bias)


def _rope_pair(u, cos_t, sin_t):
    cos = cos_t.T
    sin = sin_t.T
    tab = jnp.concatenate([cos, cos, -sin, sin], axis=1)
    lane = lax.broadcasted_iota(jnp.int32, u.shape, 1)
    t = u * tab
    return jnp.where(lane < QK_ROPE, t + pltpu.roll(t, QK_ROPE, 1), 0.0)


def _rms(x, g, eps=1e-6):
    return x * lax.rsqrt(jnp.mean(x * x, axis=-1, keepdims=True) + eps) * g


def _mla_inproj_kernel(x_ref, win_ref, wdn_ref, qg_ref, kvg_ref, cos_ref, sin_ref,
                       qn_ref, sg_ref, lat_ref, kr_ref):
    for rows in _row_slabs(x_ref.shape[0]):
        xb = x_ref[rows, :].astype(BF16)
        qn_ref[rows, :] = _rms(_dot(xb, win_ref[:, 0:Q_LORA]), qg_ref[...]).astype(BF16)
        sg_ref[rows, :] = _silu(_dot(xb, win_ref[:, Q_LORA:])).astype(BF16)
        c = _dot(xb, wdn_ref[...])
        lat_ref[rows, :] = _rms(c[:, 0:KV_LORA], kvg_ref[...]).astype(BF16)
        kr_ref[rows, :] = _rope_pair(c[:, KV_LORA:], cos_ref[:, rows], sin_ref[:, rows]).astype(BF16)


def _mla_inproj(x2, win, wdn, qg, kvg, cos_t, sin_t, seq):
    t = x2.shape[0]
    tm = WIDE_ROW_TILE
    nseq = seq // tm
    row = lambda i: (i, 0)
    pos = lambda i: (0, i % nseq)
    widths = (Q_LORA, WIDTH_B, KV_LORA, 2 * QK_ROPE)
    return pl.pallas_call(
        _mla_inproj_kernel,
        grid=(t // tm,),
        in_specs=[pl.BlockSpec((tm, D_MODEL), row), _resident(win.shape), _resident(wdn.shape),
                  _resident(qg.shape), _resident(kvg.shape),
                  pl.BlockSpec((QK_ROPE // 2, tm), pos), pl.BlockSpec((QK_ROPE // 2, tm), pos)],
        out_specs=[pl.BlockSpec((tm, w), row) for w in widths],
        out_shape=[jax.ShapeDtypeStruct((t, w), BF16) for w in widths],
        compiler_params=_params("arbitrary"),
        name="mla_inproj",
    )(x2, win, wdn, qg, kvg, cos_t, sin_t)


def _q_up_kernel(qn_ref, wnt_ref, wrt_ref, cos_ref, sin_ref, q_ref):
    scale = (QK_NOPE + QK_ROPE) ** -0.5 * math.log2(math.e)
    tq = ATTN_TQ
    half = QK_ROPE // 2
    zeros = jnp.zeros((QK_PAD - QK_NOPE - QK_ROPE, tq), BF16)
    for blk in range(qn_ref.shape[0] // tq):
        cols = slice(blk * tq, (blk + 1) * tq)
        qn = qn_ref[cols, :]
        cos = cos_ref[:, cols]
        sin = sin_ref[:, cols]
        yt_n = _dot_nt(wnt_ref[...], qn)
        yt_r = _dot_nt(wrt_ref[...], qn)
        for h in range(H_B):
            x1 = yt_r[h * QK_ROPE:h * QK_ROPE + half]
            x2 = yt_r[h * QK_ROPE + half:(h + 1) * QK_ROPE]
            q_ref[0, h, blk, 0:QK_NOPE, :] = (yt_n[h * QK_NOPE:(h + 1) * QK_NOPE] * scale).astype(BF16)
            q_ref[0, h, blk, QK_NOPE:QK_NOPE + half, :] = ((x1 * cos - x2 * sin) * scale).astype(BF16)
            q_ref[0, h, blk, QK_NOPE + half:QK_NOPE + QK_ROPE, :] = ((x2 * cos + x1 * sin) * scale).astype(BF16)
            q_ref[0, h, blk, QK_NOPE + QK_ROPE:, :] = zeros


def _q_up(qn, wnt, wrt, cos_t, sin_t, batch, seq):
    tm = WIDE_ROW_TILE
    tq = ATTN_TQ
    nseq = seq // tm
    pos = lambda b, i: (0, i)
    return pl.pallas_call(
        _q_up_kernel,
        grid=(batch, nseq),
        in_specs=[pl.BlockSpec((tm, Q_LORA), lambda b, i: (b * nseq + i, 0)),
                  _resident(wnt.shape), _resident(wrt.shape),
                  pl.BlockSpec((QK_ROPE // 2, tm), pos), pl.BlockSpec((QK_ROPE // 2, tm), pos)],
        out_specs=pl.BlockSpec((1, H_B, tm // tq, QK_PAD, tq), lambda b, i: (b, 0, i, 0, 0)),
        out_shape=jax.ShapeDtypeStruct((batch, H_B, seq // tq, QK_PAD, tq), BF16),
        compiler_params=_params("arbitrary", "arbitrary"),
        name="q_up",
    )(qn, wnt, wrt, cos_t, sin_t)


KV_RING = 3


def _kv_up_kernel(lat_hbm, kr_hbm, wk_ref, wvt_ref, k_ref, vt_ref, lat_buf, kr_buf, sem):
    tk = ATTN_TK
    tm = lat_buf.shape[1]
    step = pl.program_id(0) * pl.num_programs(1) + pl.program_id(1)
    total = pl.num_programs(0) * pl.num_programs(1)

    def copies(s):
        slot = s % KV_RING
        rows = pl.ds(pl.multiple_of(s * tm, tm), tm)
        return (pltpu.make_async_copy(lat_hbm.at[rows], lat_buf.at[slot], sem.at[0, slot]),
                pltpu.make_async_copy(kr_hbm.at[rows], kr_buf.at[slot], sem.at[1, slot]))

    def start(s):
        for c in copies(s):
            c.start()

    @pl.when(step == 0)
    def _():
        start(0)

    @pl.when(jnp.logical_and(step == 0, total > 1))
    def _():
        start(1)

    @pl.when(step + KV_RING - 1 < total)
    def _():
        start(step + KV_RING - 1)

    for c in copies(step):
        c.wait()
    lat_ref = lat_buf.at[step % KV_RING]
    kr_ref = kr_buf.at[step % KV_RING]

    for blk in range(tm // tk):
        rows = slice(blk * tk, (blk + 1) * tk)
        lat = lat_ref[rows, :]
        kn = _dot(lat, wk_ref[...]).astype(BF16)
        kr = kr_ref[rows, :]
        for h in range(H_B):
            k_ref[0, h, rows, 0:QK_NOPE] = kn[:, h * QK_NOPE:(h + 1) * QK_NOPE]
            k_ref[0, h, rows, QK_NOPE:] = kr
        vt = _dot_nt(wvt_ref[...], lat).astype(BF16)
        vt_ref[0, :, blk, 0:V_HEAD, :] = vt.reshape(H_B, V_HEAD, tk)
        extra = lax.broadcasted_iota(jnp.int32, (H_B, V_ROWS - V_HEAD, tk), 1) == 0
        vt_ref[0, :, blk, V_HEAD:, :] = extra.astype(BF16)


def _kv_up(lat, kr, wk, wvt, batch, seq):
    tk = ATTN_TK
    tm = WIDE_ROW_TILE
    nk = seq // tk
    nstep = seq // tm
    return pl.pallas_call(
        _kv_up_kernel,
        grid=(batch, nstep),
        in_specs=[pl.BlockSpec(memory_space=pl.ANY), pl.BlockSpec(memory_space=pl.ANY),
                  _resident(wk.shape), _resident(wvt.shape)],
        out_specs=[pl.BlockSpec((1, H_B, tm, QK_PAD), lambda b, i: (b, 0, i, 0)),
                   pl.BlockSpec((1, H_B, tm // tk, V_ROWS, tk), lambda b, i: (b, 0, i, 0, 0))],
        out_shape=[jax.ShapeDtypeStruct((batch, H_B, seq, QK_PAD), BF16),
                   jax.ShapeDtypeStruct((batch, H_B, nk, V_ROWS, tk), BF16)],
        scratch_shapes=[pltpu.VMEM((KV_RING, tm, KV_LORA), BF16),
                        pltpu.VMEM((KV_RING, tm, 2 * QK_ROPE), BF16),
                        pltpu.SemaphoreType.DMA((2, KV_RING))],
        compiler_params=_params("arbitrary", "arbitrary"),
        name="kv_up",
    )(lat, kr, wk, wvt)


def _attn_kernel(q_ref, k_ref, vt_ref, sg_ref, o_ref, sa_ref, sb_ref, mxa_ref, mxb_ref, m_ref, acc_ref,
                 hold_ref):
    tq = ATTN_TQ
    tk = ATTN_TK
    npairs = q_ref.shape[2] // 2
    last = npairs - 1

    def q_rows(pair, row):
        return q_ref[0, 0, 2 * pair + row]

    def scores(q, j, nblk, s_ref, mx_ref, off=0):
        kb = k_ref[0, 0, pl.ds(pl.multiple_of(j * tk, tk), nblk * tk), :]
        s = _dot(kb, q)
        s_ref[off:off + nblk * tk, :] = s
        mx_ref[...] = jnp.max(s, axis=0, keepdims=True)

    def update(row, j, nblk, s_ref, mx_ref, diag_at=None, off=0):
        s = s_ref[off:off + nblk * tk, :]
        if diag_at is None:
            block_max = mx_ref[...]
        else:
            kpos = lax.broadcasted_iota(jnp.int32, s.shape, 0)
            qpos = lax.broadcasted_iota(jnp.int32, s.shape, 1)
            s = jnp.where(kpos <= qpos + diag_at, s, MASK_VALUE)
            block_max = jnp.max(s, axis=0, keepdims=True)
        m = m_ref[row]
        m_new = jnp.maximum(m, block_max)
        alpha = jnp.exp2(m - m_new)
        p = jnp.exp2(s - m_new).astype(BF16)
        m_ref[row] = m_new
        pv = _dot(vt_ref[0, 0, j], p[0:tk])
        for i in range(1, nblk):
            pv += _dot(vt_ref[0, 0, j + i], p[i * tk:(i + 1) * tk])
        acc_ref[row] = alpha * acc_ref[row] + pv

    def finalize(pair, row, acc):
        o = (acc[0:V_HEAD, :] / acc[V_HEAD:V_HEAD + 1, :]).T
        rows = pl.ds(pl.multiple_of((2 * pair + row) * tq, tq), tq)
        o_ref[0, rows, :] = (o * sg_ref[0, rows, :].astype(F32)).astype(BF16)

    def row_pair(r, carry):
        m_ref[...] = jnp.full_like(m_ref, MASK_VALUE)
        acc_ref[...] = jnp.zeros_like(acc_ref)

        @pl.when(r == 0)
        def _():
            scores(q_rows(r, 0), 0, 1, sa_ref, mxa_ref, off=tk)
            hold_ref[...] = jnp.ones_like(hold_ref)

        @pl.when(r > 0)
        def _():
            def step(t):
                j = 2 * t
                scores(q_rows(r, 1), j, 2, sb_ref, mxb_ref)
                update(0, j, 2, sa_ref, mxa_ref)
                scores(q_rows(r, 0), jnp.minimum(j + 2, 2 * r - 1), 2, sa_ref, mxa_ref)
                update(1, j, 2, sb_ref, mxb_ref)

            def body(i, carry):
                step(2 * i)
                step(2 * i + 1)
                return carry

            lax.fori_loop(0, r // 2, body, 0)

            @pl.when(r % 2 == 1)
            def _():
                step(r - 1)

        finalize(jnp.maximum(r - 1, 0), 1, hold_ref[...])
        scores(q_rows(r, 1), 2 * r, 2, sb_ref, mxb_ref)
        update(0, 2 * r, 1, sa_ref, mxa_ref, diag_at=0, off=tk)
        scores(q_rows(jnp.minimum(r + 1, last), 0), 0, 2, sa_ref, mxa_ref)
        update(1, 2 * r, 2, sb_ref, mxb_ref, diag_at=tk)
        finalize(r, 0, acc_ref[0])
        hold_ref[...] = acc_ref[1]
        return carry

    lax.fori_loop(0, npairs, row_pair, 0)
    finalize(last, 1, hold_ref[...])


def _attention(q, k, vt, sg3):
    b, h, s, _ = k.shape
    tq = ATTN_TQ
    assert tq == ATTN_TK
    nk = s // ATTN_TK
    head = lambda bi, hi: (bi, hi, 0, 0)
    oblk = lambda bi, hi: (bi, 0, hi)
    return pl.pallas_call(
        _attn_kernel,
        grid=(b, h),
        in_specs=[pl.BlockSpec((1, 1, s // tq, QK_PAD, tq), lambda bi, hi: (bi, hi, 0, 0, 0)),
                  pl.BlockSpec((1, 1, s, QK_PAD), head),
                  pl.BlockSpec((1, 1, nk, V_ROWS, ATTN_TK), lambda bi, hi: (bi, hi, 0, 0, 0)),
                  pl.BlockSpec((1, s, V_HEAD), oblk)],
        out_specs=pl.BlockSpec((1, s, V_HEAD), oblk),
        out_shape=jax.ShapeDtypeStruct((b, s, WIDTH_B), BF16),
        scratch_shapes=[pltpu.VMEM((2 * ATTN_TK, tq), F32), pltpu.VMEM((2 * ATTN_TK, tq), F32),
                        pltpu.VMEM((1, tq), F32), pltpu.VMEM((1, tq), F32),
                        pltpu.VMEM((2, 1, tq), F32), pltpu.VMEM((2, V_ROWS, tq), F32),
                        pltpu.VMEM((V_ROWS, tq), F32)],
        compiler_params=_params("arbitrary", "arbitrary"),
        name="mla_attention",
    )(q, k, vt, sg3)


def _rope_angles(seq, half, base):
    inv = base ** (-jnp.arange(half, dtype=F32) / half)
    return jnp.arange(seq, dtype=F32)[:, None] * inv[None, :]


def _retention_decays(c):
    lg = jnp.log1p(-jnp.exp2(-5.0 - jnp.arange(H_A, dtype=F32)))
    idx = jnp.arange(c, dtype=F32)
    up = jnp.exp((idx + 1.0)[:, None] * lg[None, :])
    down = jnp.exp(-(idx + 1.0)[:, None] * lg[None, :]) * DK_A ** -0.5
    widen = lambda t: jnp.repeat(t, DK_A, axis=1)
    causal = (idx[:, None] >= idx[None, :]).astype(F32)
    cdec = jnp.broadcast_to(jnp.exp(c * lg)[:, None, None], (H_A, 1, DV_A))
    return widen(up), widen(down), causal, cdec


def _swap_halves(w):
    half = w.shape[-1] // 2
    return jnp.concatenate([w[..., half:], w[..., :half]], axis=-1)


def kernel(x, a_w_in, a_w_out, b_w_in, b_q_norm, b_w_uq, b_w_out, kv_w_down, kv_norm, kv_w_up,
           ln_g, ln_b):
    batch, seq, _ = x.shape
    t = batch * seq
    x2 = x.reshape(t, D_MODEL)

    ang_a = _rope_angles(seq, DK_A // 2, ROPE_BASE_A)
    cos_a, sin_a = jnp.cos(ang_a), jnp.sin(ang_a)
    ang_t = _rope_angles(seq, QK_ROPE // 2, ROPE_BASE_B).T
    cos_t, sin_t = jnp.cos(ang_t), jnp.sin(ang_t)

    qdec, kdec, causal, cdec = _retention_decays(RET_CHUNK)
    o = _retention_layer(x2, a_w_in[0].astype(BF16), cos_a, sin_a, qdec, kdec, causal, cdec, seq)
    x1 = _outproj_ln(o, a_w_out[0].astype(BF16), x2,
                     ln_g[0][None, :], ln_b[0][None, :])

    w_lat = kv_w_down[:, :KV_LORA].astype(BF16)
    w_rope = kv_w_down[:, KV_LORA:]
    w_rope = jnp.concatenate([w_rope, _swap_halves(w_rope)], axis=-1).astype(BF16)
    wdn = jnp.concatenate([w_lat, w_rope], axis=-1)
    qn, sgb, lat, kr = _mla_inproj(x1, b_w_in[0].astype(BF16), wdn,
                                   b_q_norm[0][None, :], kv_norm[None, :], cos_t, sin_t, seq)

    wuq_t = b_w_uq[0].T.reshape(H_B, QK_NOPE + QK_ROPE, Q_LORA)
    wuq_nt = wuq_t[:, :QK_NOPE].reshape(H_B * QK_NOPE, Q_LORA).astype(BF16)
    wuq_rt = wuq_t[:, QK_NOPE:].reshape(H_B * QK_ROPE, Q_LORA).astype(BF16)
    qh = _q_up(qn, wuq_nt, wuq_rt, cos_t, sin_t, batch, seq)

    wup = kv_w_up.reshape(KV_LORA, H_B, QK_NOPE + V_HEAD)
    wk = wup[..., :QK_NOPE].reshape(KV_LORA, H_B * QK_NOPE).astype(BF16)
    wvt = wup[..., QK_NOPE:].reshape(KV_LORA, H_B * V_HEAD).T.astype(BF16)
    kh, vt = _kv_up(lat, kr, wk, wvt, batch, seq)

    ob = _attention(qh, kh, vt, sgb.reshape(batch, seq, WIDTH_B))
    out = _outproj_ln(ob.reshape(t, WIDTH_B), b_w_out[0].astype(BF16), x1,
                      ln_g[1][None, :], ln_b[1][None, :])
    return out.reshape(batch, seq, D_MODEL)
```

```python
import functools
import math

import jax
import jax.numpy as jnp
from jax import lax
from jax.experimental import pallas as pl
from jax.experimental.pallas import tpu as pltpu

D_MODEL = 1024
DEPTH = 2

H_A = 4
DK_A = D_MODEL // H_A
DV_A = 2 * DK_A
WIDTH_A = H_A * DV_A
ROPE_BASE_A = 10000.0

H_B = 16
QK_NOPE = 128
QK_ROPE = 64
V_HEAD = 128
Q_LORA = 768
KV_LORA = 512
WIDTH_B = H_B * V_HEAD
ROPE_BASE_B = 10000.0
QK_PAD = 256
BF16_SUBLANE_TILE = 16
V_ROWS = V_HEAD + BF16_SUBLANE_TILE

DEEPNORM_ALPHA = (2.0 * DEPTH) ** 0.25

RET_CHUNK = 256
ROW_TILE = 512
WIDE_ROW_TILE = 1024
ROW_SLAB = 256
ATTN_TQ = 512
ATTN_TK = 512
MASK_VALUE = -1e30

VMEM_LIMIT_BYTES = 56 * 1024 * 1024

BF16 = jnp.bfloat16
F32 = jnp.float32


def _params(*semantics):
    return pltpu.CompilerParams(dimension_semantics=semantics,
                                vmem_limit_bytes=VMEM_LIMIT_BYTES)


def _resident(shape):
    zeros = (0,) * len(shape)
    return pl.BlockSpec(shape, lambda *_: zeros, pipeline_mode=pl.Buffered(1))


def _dot(a, b):
    return jnp.dot(a, b, preferred_element_type=F32)


def _dot_nt(a, b):
    return lax.dot_general(a, b, (((1,), (1,)), ((), ())), preferred_element_type=F32)


def _dot_tn(a, b):
    return lax.dot_general(a, b, (((0,), (0,)), ((), ())), preferred_element_type=F32)


def _row_slabs(rows):
    return [slice(i, i + ROW_SLAB) for i in range(0, rows, ROW_SLAB)]


def _silu(g):
    return g / (1.0 + jnp.exp(-g))


def _retention_kernel(x_ref, w_ref, cos_ref, sin_ref, qdec_ref, kdec_ref, causal_ref, cdec_ref,
                      wout_ref, g_ref, b_ref, o_ref, state_ref, sbf_ref, *, steps_per_seq):
    assert ROW_SLAB == RET_CHUNK
    qk = H_A * DK_A
    half = DK_A // 2

    @pl.when(pl.program_id(0) % steps_per_seq == 0)
    def _():
        state_ref[...] = jnp.zeros_like(state_ref)
        sbf_ref[...] = jnp.zeros_like(sbf_ref)

    def rope(h, i, cos, sin, dec_ref):
        lo = slice(i * DK_A, i * DK_A + half)
        hi = slice(i * DK_A + half, (i + 1) * DK_A)
        x1 = h[:, lo]
        x2 = h[:, hi]
        r1 = ((x1 * cos - x2 * sin) * dec_ref[:, lo]).astype(BF16)
        r2 = ((x2 * cos + x1 * sin) * dec_ref[:, hi]).astype(BF16)
        return jnp.concatenate([r1, r2], axis=1)

    for rows in _row_slabs(x_ref.shape[0]):
        xb = x_ref[rows, :].astype(BF16)
        cos = cos_ref[rows, :]
        sin = sin_ref[rows, :]
        hq = _dot(xb, w_ref[:, 0:qk])
        hk = _dot(xb, w_ref[:, qk:2 * qk])
        v_all = _dot(xb, w_ref[:, 2 * qk:2 * qk + WIDTH_A]).astype(BF16)
        sg_all = _silu(_dot(xb, w_ref[:, 2 * qk + WIDTH_A:]))
        gated = []
        for h in range(H_A):
            v_cols = slice(h * DV_A, (h + 1) * DV_A)
            q = rope(hq, h, cos, sin, qdec_ref)
            k = rope(hk, h, cos, sin, kdec_ref)
            v = v_all[:, v_cols]
            scores = (_dot_nt(q, k) * causal_ref[...]).astype(BF16)
            o = _dot(scores, v) + _dot(q, sbf_ref[h])
            state = (state_ref[h] + _dot_tn(k, v)) * cdec_ref[h]
            state_ref[h] = state
            sbf_ref[h] = state.astype(BF16)

            mu = jnp.mean(o, axis=-1, keepdims=True)
            d = o - mu
            var = jnp.mean(d * d, axis=-1, keepdims=True)
            gated.append((d * lax.rsqrt(var + 1e-5) * sg_all[:, v_cols]).astype(BF16))

        z = DEEPNORM_ALPHA * x_ref[rows, :] + _dot(jnp.concatenate(gated, axis=1), wout_ref[...])
        mu = jnp.mean(z, axis=-1, keepdims=True)
        d = z - mu
        var = jnp.mean(d * d, axis=-1, keepdims=True)
        o_ref[rows, :] = d * lax.rsqrt(var + 1e-5) * g_ref[...] + b_ref[...]


def _retention_layer(x2, w, cos, sin, qdec, kdec, causal, cdec, wout, g, bias, seq):
    t = x2.shape[0]
    tm = WIDE_ROW_TILE
    nseq = seq // tm
    row = lambda i: (i, 0)
    pos = lambda i: (i % nseq, 0)
    return pl.pallas_call(
        functools.partial(_retention_kernel, steps_per_seq=nseq),
        grid=(t // tm,),
        in_specs=[pl.BlockSpec((tm, D_MODEL), row),
                  _resident(w.shape),
                  pl.BlockSpec((tm, DK_A // 2), pos),
                  pl.BlockSpec((tm, DK_A // 2), pos),
                  _resident(qdec.shape), _resident(kdec.shape),
                  _resident(causal.shape), _resident(cdec.shape),
                  _resident(wout.shape), _resident(g.shape), _resident(bias.shape)],
        out_specs=pl.BlockSpec((tm, D_MODEL), row),
        out_shape=jax.ShapeDtypeStruct((t, D_MODEL), F32),
        scratch_shapes=[pltpu.VMEM((H_A, DK_A, DV_A), F32), pltpu.VMEM((H_A, DK_A, DV_A), BF16)],
        compiler_params=_params("arbitrary"),
        name="retention_layer",
    )(x2, w, cos, sin, qdec, kdec, causal, cdec, wout, g, bias)


def _outproj_ln_kernel(a_ref, w_ref, x_ref, g_ref, b_ref, o_ref):
    for rows in _row_slabs(a_ref.shape[0]):
        z = DEEPNORM_ALPHA * x_ref[rows, :] + _dot(a_ref[rows, :], w_ref[...])
        mu = jnp.mean(z, axis=-1, keepdims=True)
        d = z - mu
        var = jnp.mean(d * d, axis=-1, keepdims=True)
        o_ref[rows, :] = d * lax.rsqrt(var + 1e-5) * g_ref[...] + b_ref[...]


def _outproj_ln(a, w, x2, g, bias):
    t, width = a.shape
    tm = WIDE_ROW_TILE
    row = lambda i: (i, 0)
    return pl.pallas_call(
        _outproj_ln_kernel,
        grid=(t // tm,),
        in_specs=[pl.BlockSpec((tm, width), row), _resident(w.shape),
                  pl.BlockSpec((tm, D_MODEL), row), _resident(g.shape), _resident(bias.shape)],
        out_specs=pl.BlockSpec((tm, D_MODEL), row),
        out_shape=jax.ShapeDtypeStruct((t, D_MODEL), F32),
        compiler_params=_params("arbitrary"),
        name="outproj_ln",
    )(a, w, x2, g, bias)


def _rope_pair(u, cos_t, sin_t):
    cos = cos_t.T
    sin = sin_t.T
    tab = jnp.concatenate([cos, cos, -sin, sin], axis=1)
    lane = lax.broadcasted_iota(jnp.int32, u.shape, 1)
    t = u * tab
    return jnp.where(lane < QK_ROPE, t + pltpu.roll(t, QK_ROPE, 1), 0.0)


def _rms(x, g, eps=1e-6):
    return x * lax.rsqrt(jnp.mean(x * x, axis=-1, keepdims=True) + eps) * g


def _mla_inproj_kernel(x_ref, win_ref, wdn_ref, qg_ref, kvg_ref, cos_ref, sin_ref,
                       qn_ref, sg_ref, lat_ref, kr_ref):
    for rows in _row_slabs(x_ref.shape[0]):
        xb = x_ref[rows, :].astype(BF16)
        qn_ref[rows, :] = _rms(_dot(xb, win_ref[:, 0:Q_LORA]), qg_ref[...]).astype(BF16)
        sg_ref[rows, :] = _silu(_dot(xb, win_ref[:, Q_LORA:])).astype(BF16)
        c = _dot(xb, wdn_ref[...])
        lat_ref[rows, :] = _rms(c[:, 0:KV_LORA], kvg_ref[...]).astype(BF16)
        kr_ref[rows, :] = _rope_pair(c[:, KV_LORA:], cos_ref[:, rows], sin_ref[:, rows]).astype(BF16)


def _mla_inproj(x2, win, wdn, qg, kvg, cos_t, sin_t, seq):
    t = x2.shape[0]
    tm = WIDE_ROW_TILE
    nseq = seq // tm
    row = lambda i: (i, 0)
    pos = lambda i: (0, i % nseq)
    widths = (Q_LORA, WIDTH_B, KV_LORA, 2 * QK_ROPE)
    return pl.pallas_call(
        _mla_inproj_kernel,
        grid=(t // tm,),
        in_specs=[pl.BlockSpec((tm, D_MODEL), row), _resident(win.shape), _resident(wdn.shape),
                  _resident(qg.shape), _resident(kvg.shape),
                  pl.BlockSpec((QK_ROPE // 2, tm), pos), pl.BlockSpec((QK_ROPE // 2, tm), pos)],
        out_specs=[pl.BlockSpec((tm, w), row) for w in widths],
        out_shape=[jax.ShapeDtypeStruct((t, w), BF16) for w in widths],
        compiler_params=_params("arbitrary"),
        name="mla_inproj",
    )(x2, win, wdn, qg, kvg, cos_t, sin_t)


def _q_up_kernel(qn_ref, wnt_ref, wrt_ref, cos_ref, sin_ref, q_ref):
    scale = (QK_NOPE + QK_ROPE) ** -0.5 * math.log2(math.e)
    tq = ATTN_TQ
    half = QK_ROPE // 2
    zeros = jnp.zeros((QK_PAD - QK_NOPE - QK_ROPE, tq), BF16)
    for blk in range(qn_ref.shape[0] // tq):
        cols = slice(blk * tq, (blk + 1) * tq)
        qn = qn_ref[cols, :]
        cos = cos_ref[:, cols]
        sin = sin_ref[:, cols]
        yt_n = _dot_nt(wnt_ref[...], qn)
        yt_r = _dot_nt(wrt_ref[...], qn)
        for h in range(H_B):
            x1 = yt_r[h * QK_ROPE:h * QK_ROPE + half]
            x2 = yt_r[h * QK_ROPE + half:(h + 1) * QK_ROPE]
            q_ref[0, h, blk, 0:QK_NOPE, :] = (yt_n[h * QK_NOPE:(h + 1) * QK_NOPE] * scale).astype(BF16)
            q_ref[0, h, blk, QK_NOPE:QK_NOPE + half, :] = ((x1 * cos - x2 * sin) * scale).astype(BF16)
            q_ref[0, h, blk, QK_NOPE + half:QK_NOPE + QK_ROPE, :] = ((x2 * cos + x1 * sin) * scale).astype(BF16)
            q_ref[0, h, blk, QK_NOPE + QK_ROPE:, :] = zeros


def _q_up(qn, wnt, wrt, cos_t, sin_t, batch, seq):
    tm = WIDE_ROW_TILE
    tq = ATTN_TQ
    nseq = seq // tm
    pos = lambda b, i: (0, i)
    return pl.pallas_call(
        _q_up_kernel,
        grid=(batch, nseq),
        in_specs=[pl.BlockSpec((tm, Q_LORA), lambda b, i: (b * nseq + i, 0)),
                  _resident(wnt.shape), _resident(wrt.shape),
                  pl.BlockSpec((QK_ROPE // 2, tm), pos), pl.BlockSpec((QK_ROPE // 2, tm), pos)],
        out_specs=pl.BlockSpec((1, H_B, tm // tq, QK_PAD, tq), lambda b, i: (b, 0, i, 0, 0)),
        out_shape=jax.ShapeDtypeStruct((batch, H_B, seq // tq, QK_PAD, tq), BF16),
        compiler_params=_params("arbitrary", "arbitrary"),
        name="q_up",
    )(qn, wnt, wrt, cos_t, sin_t)


def _kv_up_kernel(lat_ref, kr_ref, wk_ref, wvt_ref, k_ref, vt_ref):
    tk = ATTN_TK
    for blk in range(lat_ref.shape[0] // tk):
        rows = slice(blk * tk, (blk + 1) * tk)
        lat = lat_ref[rows, :]
        kn = _dot(lat, wk_ref[...]).astype(BF16)
        kr = kr_ref[rows, :]
        for h in range(H_B):
            k_ref[0, h, rows, 0:QK_NOPE] = kn[:, h * QK_NOPE:(h + 1) * QK_NOPE]
            k_ref[0, h, rows, QK_NOPE:] = kr
        vt = _dot_nt(wvt_ref[...], lat).astype(BF16)
        vt_ref[0, :, blk, 0:V_HEAD, :] = vt.reshape(H_B, V_HEAD, tk)
        extra = lax.broadcasted_iota(jnp.int32, (H_B, V_ROWS - V_HEAD, tk), 1) == 0
        vt_ref[0, :, blk, V_HEAD:, :] = extra.astype(BF16)


def _kv_up(lat, kr, wk, wvt, batch, seq):
    tk = ATTN_TK
    tm = WIDE_ROW_TILE
    nk = seq // tk
    nstep = seq // tm
    row = lambda b, i: (b * nstep + i, 0)
    return pl.pallas_call(
        _kv_up_kernel,
        grid=(batch, nstep),
        in_specs=[pl.BlockSpec((tm, KV_LORA), row), pl.BlockSpec((tm, 2 * QK_ROPE), row),
                  _resident(wk.shape), _resident(wvt.shape)],
        out_specs=[pl.BlockSpec((1, H_B, tm, QK_PAD), lambda b, i: (b, 0, i, 0)),
                   pl.BlockSpec((1, H_B, tm // tk, V_ROWS, tk), lambda b, i: (b, 0, i, 0, 0))],
        out_shape=[jax.ShapeDtypeStruct((batch, H_B, seq, QK_PAD), BF16),
                   jax.ShapeDtypeStruct((batch, H_B, nk, V_ROWS, tk), BF16)],
        compiler_params=_params("arbitrary", "arbitrary"),
        name="kv_up",
    )(lat, kr, wk, wvt)


def _attn_kernel(q_ref, k_ref, vt_ref, sg_ref, o_ref, sa_ref, sb_ref, mxa_ref, mxb_ref, m_ref, acc_ref,
                 hold_ref):
    tq = ATTN_TQ
    tk = ATTN_TK
    npairs = q_ref.shape[2] // 2
    last = npairs - 1

    def q_rows(pair, row):
        return q_ref[0, 0, 2 * pair + row]

    def scores(q, j, nblk, s_ref, mx_ref, off=0):
        kb = k_ref[0, 0, pl.ds(pl.multiple_of(j * tk, tk), nblk * tk), :]
        s = _dot(kb, q)
        s_ref[off:off + nblk * tk, :] = s
        mx_ref[...] = jnp.max(s, axis=0, keepdims=True)

    def update(row, j, nblk, s_ref, mx_ref, diag_at=None, off=0):
        s = s_ref[off:off + nblk * tk, :]
        if diag_at is None:
            block_max = mx_ref[...]
        else:
            kpos = lax.broadcasted_iota(jnp.int32, s.shape, 0)
            qpos = lax.broadcasted_iota(jnp.int32, s.shape, 1)
            s = jnp.where(kpos <= qpos + diag_at, s, MASK_VALUE)
            block_max = jnp.max(s, axis=0, keepdims=True)
        m = m_ref[row]
        m_new = jnp.maximum(m, block_max)
        alpha = jnp.exp2(m - m_new)
        p = jnp.exp2(s - m_new).astype(BF16)
        m_ref[row] = m_new
        pv = _dot(vt_ref[0, 0, j], p[0:tk])
        for i in range(1, nblk):
            pv += _dot(vt_ref[0, 0, j + i], p[i * tk:(i + 1) * tk])
        acc_ref[row] = alpha * acc_ref[row] + pv

    def finalize(pair, row, acc):
        o = (acc[0:V_HEAD, :] / acc[V_HEAD:V_HEAD + 1, :]).T
        rows = pl.ds(pl.multiple_of((2 * pair + row) * tq, tq), tq)
        o_ref[0, rows, :] = (o * sg_ref[0, rows, :].astype(F32)).astype(BF16)

    def row_pair(r, carry):
        m_ref[...] = jnp.full_like(m_ref, MASK_VALUE)
        acc_ref[...] = jnp.zeros_like(acc_ref)

        @pl.when(r == 0)
        def _():
            scores(q_rows(r, 0), 0, 1, sa_ref, mxa_ref, off=tk)
            hold_ref[...] = jnp.ones_like(hold_ref)

        @pl.when(r > 0)
        def _():
            def step(t):
                j = 2 * t
                scores(q_rows(r, 1), j, 2, sb_ref, mxb_ref)
                update(0, j, 2, sa_ref, mxa_ref)
                scores(q_rows(r, 0), jnp.minimum(j + 2, 2 * r - 1), 2, sa_ref, mxa_ref)
                update(1, j, 2, sb_ref, mxb_ref)

            def body(i, carry):
                step(2 * i)
                step(2 * i + 1)
                return carry

            lax.fori_loop(0, r // 2, body, 0)

            @pl.when(r % 2 == 1)
            def _():
                step(r - 1)

        finalize(jnp.maximum(r - 1, 0), 1, hold_ref[...])
        scores(q_rows(r, 1), 2 * r, 2, sb_ref, mxb_ref)
        update(0, 2 * r, 1, sa_ref, mxa_ref, diag_at=0, off=tk)
        scores(q_rows(jnp.minimum(r + 1, last), 0), 0, 2, sa_ref, mxa_ref)
        update(1, 2 * r, 2, sb_ref, mxb_ref, diag_at=tk)
        finalize(r, 0, acc_ref[0])
        hold_ref[...] = acc_ref[1]
        return carry

    lax.fori_loop(0, npairs, row_pair, 0)
    finalize(last, 1, hold_ref[...])


def _attention(q, k, vt, sg3):
    b, h, s, _ = k.shape
    tq = ATTN_TQ
    assert tq == ATTN_TK
    nk = s // ATTN_TK
    head = lambda bi, hi: (bi, hi, 0, 0)
    oblk = lambda bi, hi: (bi, 0, hi)
    return pl.pallas_call(
        _attn_kernel,
        grid=(b, h),
        in_specs=[pl.BlockSpec((1, 1, s // tq, QK_PAD, tq), lambda bi, hi: (bi, hi, 0, 0, 0)),
                  pl.BlockSpec((1, 1, s, QK_PAD), head),
                  pl.BlockSpec((1, 1, nk, V_ROWS, ATTN_TK), lambda bi, hi: (bi, hi, 0, 0, 0)),
                  pl.BlockSpec((1, s, V_HEAD), oblk)],
        out_specs=pl.BlockSpec((1, s, V_HEAD), oblk),
        out_shape=jax.ShapeDtypeStruct((b, s, WIDTH_B), BF16),
        scratch_shapes=[pltpu.VMEM((2 * ATTN_TK, tq), F32), pltpu.VMEM((2 * ATTN_TK, tq), F32),
                        pltpu.VMEM((1, tq), F32), pltpu.VMEM((1, tq), F32),
                        pltpu.VMEM((2, 1, tq), F32), pltpu.VMEM((2, V_ROWS, tq), F32),
                        pltpu.VMEM((V_ROWS, tq), F32)],
        compiler_params=_params("arbitrary", "arbitrary"),
        name="mla_attention",
    )(q, k, vt, sg3)


def _rope_angles(seq, half, base):
    inv = base ** (-jnp.arange(half, dtype=F32) / half)
    return jnp.arange(seq, dtype=F32)[:, None] * inv[None, :]


def _retention_decays(c):
    lg = jnp.log1p(-jnp.exp2(-5.0 - jnp.arange(H_A, dtype=F32)))
    idx = jnp.arange(c, dtype=F32)
    up = jnp.exp((idx + 1.0)[:, None] * lg[None, :])
    down = jnp.exp(-(idx + 1.0)[:, None] * lg[None, :]) * DK_A ** -0.5
    widen = lambda t: jnp.repeat(t, DK_A, axis=1)
    causal = (idx[:, None] >= idx[None, :]).astype(F32)
    cdec = jnp.broadcast_to(jnp.exp(c * lg)[:, None, None], (H_A, 1, DV_A))
    return widen(up), widen(down), causal, cdec


def _swap_halves(w):
    half = w.shape[-1] // 2
    return jnp.concatenate([w[..., half:], w[..., :half]], axis=-1)


def kernel(x, a_w_in, a_w_out, b_w_in, b_q_norm, b_w_uq, b_w_out, kv_w_down, kv_norm, kv_w_up,
           ln_g, ln_b):
    batch, seq, _ = x.shape
    t = batch * seq
    x2 = x.reshape(t, D_MODEL)

    ang_a = _rope_angles(seq, DK_A // 2, ROPE_BASE_A)
    cos_a, sin_a = jnp.cos(ang_a), jnp.sin(ang_a)
    ang_t = _rope_angles(seq, QK_ROPE // 2, ROPE_BASE_B).T
    cos_t, sin_t = jnp.cos(ang_t), jnp.sin(ang_t)

    qdec, kdec, causal, cdec = _retention_decays(RET_CHUNK)
    x1 = _retention_layer(x2, a_w_in[0].astype(BF16), cos_a, sin_a, qdec, kdec, causal, cdec,
                          a_w_out[0].astype(BF16), ln_g[0][None, :], ln_b[0][None, :], seq)

    w_lat = kv_w_down[:, :KV_LORA].astype(BF16)
    w_rope = kv_w_down[:, KV_LORA:]
    w_rope = jnp.concatenate([w_rope, _swap_halves(w_rope)], axis=-1).astype(BF16)
    wdn = jnp.concatenate([w_lat, w_rope], axis=-1)
    qn, sgb, lat, kr = _mla_inproj(x1, b_w_in[0].astype(BF16), wdn,
                                   b_q_norm[0][None, :], kv_norm[None, :], cos_t, sin_t, seq)

    wuq_t = b_w_uq[0].T.reshape(H_B, QK_NOPE + QK_ROPE, Q_LORA)
    wuq_nt = wuq_t[:, :QK_NOPE].reshape(H_B * QK_NOPE, Q_LORA).astype(BF16)
    wuq_rt = wuq_t[:, QK_NOPE:].reshape(H_B * QK_ROPE, Q_LORA).astype(BF16)
    qh = _q_up(qn, wuq_nt, wuq_rt, cos_t, sin_t, batch, seq)

    wup = kv_w_up.reshape(KV_LORA, H_B, QK_NOPE + V_HEAD)
    wk = wup[..., :QK_NOPE].reshape(KV_LORA, H_B * QK_NOPE).astype(BF16)
    wvt = wup[..., QK_NOPE:].reshape(KV_LORA, H_B * V_HEAD).T.astype(BF16)
    kh, vt = _kv_up(lat, kr, wk, wvt, batch, seq)

    ob = _attention(qh, kh, vt, sgb.reshape(batch, seq, WIDTH_B))
    out = _outproj_ln(ob.reshape(t, WIDTH_B), b_w_out[0].astype(BF16), x1,
                      ln_g[1][None, :], ln_b[1][None, :])
    return out.reshape(batch, seq, D_MODEL)
```
